```python
import math
import jax, jax.numpy as jnp
from jax import lax
import numpy as np

D_MODEL = 2048
BATCH = 8
SEQ = 2048
DEPTH = 1
DEC_BATCH = 128
DEC_SEQ = 4
PAST_LEN = 16384
PAGE_SIZE = 128

N_META = 16
EPS = 1e-6
SSD_HEADS = 16
SSD_HEAD_DIM = 64
SSD_INNER = SSD_HEADS * SSD_HEAD_DIM
SSD_GROUPS = 2
SSD_HPG = SSD_HEADS // SSD_GROUPS
D_STATE = 128
CONV_W = 4
CONV_DIM = SSD_INNER + 2 * SSD_GROUPS * D_STATE
CHUNK = 128
MLA_HEADS = 8
Q_LORA = 512
KV_LORA = 512
QK_NOPE = 128
QK_ROPE = 64
V_DIM = 128
MLA_INNER = MLA_HEADS * V_DIM
ROPE_THETA = 10000.0
SM_SCALE = (QK_NOPE + QK_ROPE) ** -0.5
Q_BLOCK = 128
IN_WIDTH = Q_LORA + KV_LORA + QK_ROPE + SSD_INNER + CONV_DIM + SSD_HEADS
IN_SPLITS = (Q_LORA, Q_LORA + KV_LORA, Q_LORA + KV_LORA + QK_ROPE,
             Q_LORA + KV_LORA + QK_ROPE + SSD_INNER,
             Q_LORA + KV_LORA + QK_ROPE + SSD_INNER + CONV_DIM)
MIX_WIDTH = SSD_INNER + MLA_INNER
N_EXPERTS = 32
TOP_K = 4
D_FF = 2048
SWIGLU_LIMIT = 7.0
SWIGLU_ALPHA = 1.702

kernel_name = "hymba_ssd_mla_moe_step"


def rmsnorm(x, g):
    xf = x.astype(jnp.float32)
    y = xf * lax.rsqrt(jnp.mean(xf * xf, axis=-1, keepdims=True) + EPS)
    return (y * g.astype(jnp.float32)).astype(x.dtype)


def rope_tables(pos):
    inv_freq = ROPE_THETA ** (-jnp.arange(0, QK_ROPE, 2, dtype=jnp.float32) / QK_ROPE)
    ang = pos.astype(jnp.float32)[:, None] * inv_freq[None, :]
    return jnp.cos(ang), jnp.sin(ang)


def apply_rope(x, cos, sin):
    xf = x.astype(jnp.float32)
    x1, x2 = xf[..., :QK_ROPE // 2], xf[..., QK_ROPE // 2:]
    return jnp.concatenate([x1 * cos - x2 * sin, x2 * cos + x1 * sin], axis=-1).astype(x.dtype)


def mixer_inputs(h, pos, g_mix, w_in, g_q, w_q_up, g_kv, w_uk):
    b_, t = h.shape[:2]
    u = rmsnorm(h, g_mix) @ w_in
    q_a, kv_a, kpe_raw, z, xbc, dt_raw = jnp.split(u, IN_SPLITS, axis=-1)
    q = (rmsnorm(q_a, g_q) @ w_q_up).reshape(b_, t, MLA_HEADS, QK_NOPE + QK_ROPE)
    cos, sin = rope_tables(pos)
    q_pe = apply_rope(q[..., QK_NOPE:], cos[:, None], sin[:, None])
    q_lat = jnp.einsum('bthn,rhn->bthr', q[..., :QK_NOPE], w_uk)
    c_kv = rmsnorm(kv_a, g_kv)
    k_pe = apply_rope(kpe_raw, cos, sin)
    return q_lat, q_pe, c_kv, k_pe, z, xbc, dt_raw


def mla_scores(q_lat, q_pe, c_kv, k_pe):
    return (jnp.einsum('bqhr,bkr->bhqk', q_lat, c_kv, preferred_element_type=jnp.float32)
            + jnp.einsum('bqhp,bkp->bhqk', q_pe, k_pe, preferred_element_type=jnp.float32)) * SM_SCALE


def mla_attend_prompt(q_lat, q_pe, c_kv, k_pe):
    b_, t = c_kv.shape[:2]
    seq = t - N_META

    def attend(ql, qp, c, kp, q0):
        lq, lk = ql.shape[1], c.shape[1]
        s = mla_scores(ql, qp, c, kp)
        mask = (q0 + jnp.arange(lq))[:, None] >= jnp.arange(lk)[None, :]
        p = jax.nn.softmax(jnp.where(mask, s, -jnp.inf), axis=-1)
        return jnp.einsum('bhqk,bkr->bqhr', p.astype(c.dtype), c)

    o_meta = attend(q_lat[:, :N_META], q_pe[:, :N_META], c_kv[:, :N_META], k_pe[:, :N_META], 0)
    nb = seq // Q_BLOCK

    def blocks(v):
        return jnp.moveaxis(v[:, N_META:].reshape(b_, nb, Q_BLOCK, *v.shape[2:]), 1, 0)

    q0s = N_META + Q_BLOCK * jnp.arange(nb)
    o_real = lax.map(lambda a: attend(a[0], a[1], c_kv, k_pe, a[2]), (blocks(q_lat), blocks(q_pe), q0s))
    o_real = jnp.moveaxis(o_real, 0, 1).reshape(b_, seq, MLA_HEADS, KV_LORA)
    return jnp.concatenate([o_meta, o_real], axis=1)


def mla_attend_sample(q_lat, q_pe, c_new, kpe_new, pool_ckv, pool_kpe, page_table):
    b_, lq = q_lat.shape[:2]

    def absorb(carry, s, c):
        m, l, acc = carry
        m_new = jnp.maximum(m, s.max(-1))
        corr = jnp.exp(m - m_new)
        p = jnp.exp(s - m_new[..., None])
        return (m_new, l * corr + p.sum(-1),
                acc * corr[..., None] + jnp.einsum('bhqk,bkr->bhqr', p, c.astype(jnp.float32)))

    def page_step(carry, pages):
        c = pool_ckv[pages]
        kp = pool_kpe[pages]
        return absorb(carry, mla_scores(q_lat, q_pe, c, kp), c), None

    init = (jnp.full((b_, MLA_HEADS, lq), -jnp.inf, jnp.float32),
            jnp.zeros((b_, MLA_HEADS, lq), jnp.float32),
            jnp.zeros((b_, MLA_HEADS, lq, KV_LORA), jnp.float32))
    carry, _ = lax.scan(page_step, init, page_table.T)
    causal = jnp.tril(jnp.ones((lq, lq), bool))
    s_new = jnp.where(causal, mla_scores(q_lat, q_pe, c_new, kpe_new), -jnp.inf)
    _, l, acc = absorb(carry, s_new, c_new)
    o = acc / l[..., None]
    return jnp.moveaxis(o, 1, 2).astype(q_lat.dtype)


def causal_conv(xbc, hist, conv_w, conv_b):
    L = xbc.shape[1]
    xp = jnp.concatenate([hist.astype(xbc.dtype), xbc], axis=1)
    out = conv_b + conv_w[CONV_W - 1] * xp[:, CONV_W - 1:CONV_W - 1 + L]
    for k in range(CONV_W - 1):
        out = out + conv_w[k] * xp[:, k:k + L]
    return jax.nn.silu(out), xp[:, xp.shape[1] - (CONV_W - 1):]


def ssd_chunked(x, dt, a, bm, cm, s0, lc):
    b_, t = x.shape[:2]
    nc = t // lc

    def chunks(v):
        return v.reshape(b_, nc, lc, *v.shape[2:])

    x, dt, a, bm, cm = chunks(x), chunks(dt), chunks(a), chunks(bm), chunks(cm)
    acs = jnp.cumsum(a, axis=2)
    causal = jnp.tril(jnp.ones((lc, lc), bool))
    seg = acs[:, :, :, None] - acs[:, :, None, :]
    decay = jnp.exp(jnp.where(causal[:, :, None, None], seg, -jnp.inf))
    cb = jnp.einsum('bcign,bcjgn->bcijg', cm, bm)
    y_diag = jnp.einsum('bcijgk,bcjgkp->bcigkp', cb[..., None] * decay * dt[:, :, None], x)
    decay_out = jnp.exp(acs[:, :, -1:] - acs)
    states = jnp.einsum('bclgn,bclgk,bclgkp->bcgkpn', bm, decay_out * dt, x)
    chunk_decay = jnp.exp(acs[:, :, -1])

    def carry_step(s, inp):
        st, dc = inp
        return s * dc[..., None, None] + st, s

    s_final, s_in = lax.scan(carry_step, s0, (jnp.moveaxis(states, 1, 0), jnp.moveaxis(chunk_decay, 1, 0)))
    s_in = jnp.moveaxis(s_in, 0, 1)
    y_off = jnp.einsum('bclgn,bcgkpn->bclgkp', cm, s_in) * jnp.exp(acs)[..., None]
    return (y_diag + y_off).reshape(b_, t, *x.shape[3:]), s_final


def ssd_mixer(z, xbc, dt_raw, conv_hist, s0, segments, conv_w, conv_b, dt_bias, a_log, d_skip, g_ssd):
    b_, t = z.shape[:2]
    f32 = jnp.float32
    xbc, conv_state = causal_conv(xbc, conv_hist, conv_w, conv_b)
    xs, bm, cm = jnp.split(xbc.astype(f32), (SSD_INNER, SSD_INNER + SSD_GROUPS * D_STATE), axis=-1)
    xs = xs.reshape(b_, t, SSD_GROUPS, SSD_HPG, SSD_HEAD_DIM)
    bm = bm.reshape(b_, t, SSD_GROUPS, D_STATE)
    cm = cm.reshape(b_, t, SSD_GROUPS, D_STATE)
    dt = jax.nn.softplus(dt_raw.astype(f32).reshape(b_, t, SSD_GROUPS, SSD_HPG)
                         + dt_bias.astype(f32).reshape(SSD_GROUPS, SSD_HPG))
    a = dt * (-jnp.exp(a_log.astype(f32))).reshape(SSD_GROUPS, SSD_HPG)
    s = s0.astype(f32).reshape(b_, SSD_GROUPS, SSD_HPG, SSD_HEAD_DIM, D_STATE)
    ys = []
    start = 0
    for seg_len, lc in segments:
        sl = slice(start, start + seg_len)
        y, s = ssd_chunked(xs[:, sl], dt[:, sl], a[:, sl], bm[:, sl], cm[:, sl], s, lc)
        ys.append(y)
        start += seg_len
    y = jnp.concatenate(ys, axis=1)
    y = y + d_skip.astype(f32).reshape(SSD_GROUPS, SSD_HPG)[..., None] * xs
    yg = (y.reshape(b_, t, SSD_INNER) * jax.nn.silu(z.astype(f32))).reshape(b_, t, SSD_GROUPS, SSD_INNER // SSD_GROUPS)
    yg = yg * lax.rsqrt(jnp.mean(yg * yg, axis=-1, keepdims=True) + EPS)
    y = yg.reshape(b_, t, SSD_INNER) * g_ssd.astype(f32)
    return (y.astype(z.dtype),
            s.reshape(b_, SSD_HEADS, SSD_HEAD_DIM, D_STATE).astype(s0.dtype),
            conv_state.astype(conv_hist.dtype))


def mixer_merge(y_ssd, o_lat, w_uv, g_attn_out, w_out):
    b_, t = o_lat.shape[:2]
    o = jnp.einsum('bthr,rhv->bthv', o_lat, w_uv).reshape(b_, t, MLA_INNER)
    o = rmsnorm(o, g_attn_out)
    return jnp.concatenate([y_ssd, o], axis=-1) @ w_out


def moe(x, w_router, b_router, w_gate, b_gate, w_up, b_up, w_down, b_down):
    logits = (x @ w_router + b_router).astype(jnp.float32)
    top_v, top_i = lax.top_k(logits, TOP_K)
    gates = jax.nn.softmax(top_v, axis=-1)
    dense_gate = jnp.sum(jax.nn.one_hot(top_i, N_EXPERTS, dtype=jnp.float32) * gates[..., None], axis=1)
    out = jnp.zeros(x.shape, jnp.float32)
    for e in range(N_EXPERTS):
        g = jnp.minimum(x @ w_gate[e] + b_gate[e], SWIGLU_LIMIT)
        u = jnp.clip(x @ w_up[e] + b_up[e], -SWIGLU_LIMIT, SWIGLU_LIMIT)
        hdn = (u + 1.0) * (g * jax.nn.sigmoid(SWIGLU_ALPHA * g))
        out = out + dense_gate[:, e:e + 1] * (hdn @ w_down[e] + b_down[e]).astype(jnp.float32)
    return out.astype(x.dtype)


def setup_inputs(seed: int = 0) -> dict:
    key = jax.random.key(seed)
    ks = jax.random.split(key, 40)
    f32 = jnp.float32
    n_pages = PAST_LEN // PAGE_SIZE
    n_used = DEC_BATCH * n_pages
    n_pool = n_used + n_used // 4

    def nrm(k, shape, scale):
        return jax.random.normal(k, shape, f32) * scale

    def gain(k, shape):
        return 1.0 + 0.02 * jax.random.normal(k, shape, f32)

    page_table = jax.random.permutation(ks[6], n_pool)[:n_used].reshape(DEC_BATCH, n_pages).astype(jnp.int32)
    dt0 = jnp.exp(jax.random.uniform(ks[19], (DEPTH, SSD_HEADS), f32) * (math.log(0.1) - math.log(0.001)) + math.log(0.001))
    dt_bias = dt0 + jnp.log(-jnp.expm1(-dt0))
    a_log = jnp.log(jax.random.uniform(ks[20], (DEPTH, SSD_HEADS), f32, 1.0, 16.0))
    return {
        "x_prompt": nrm(ks[0], (BATCH, SEQ, D_MODEL), 1.0),
        "x_sample": nrm(ks[1], (DEC_BATCH, DEC_SEQ, D_MODEL), 1.0),
        "cache_ckv": nrm(ks[2], (DEPTH, n_pool, PAGE_SIZE, KV_LORA), 1.0),
        "cache_kpe": nrm(ks[3], (DEPTH, n_pool, PAGE_SIZE, QK_ROPE), 1.0),
        "state_ssm": nrm(ks[4], (DEPTH, DEC_BATCH, SSD_HEADS, SSD_HEAD_DIM, D_STATE), 0.5),
        "state_conv": nrm(ks[5], (DEPTH, DEC_BATCH, CONV_W - 1, CONV_DIM), 1.0),
        "page_table": page_table,
        "meta_tokens": nrm(ks[7], (N_META, D_MODEL), 1.0),
        "g_mix_norm": gain(ks[8], (DEPTH, D_MODEL)),
        "w_in": nrm(ks[9], (DEPTH, D_MODEL, IN_WIDTH), D_MODEL ** -0.5),
        "g_q_norm": gain(ks[10], (DEPTH, Q_LORA)),
        "w_q_up": nrm(ks[11], (DEPTH, Q_LORA, MLA_HEADS * (QK_NOPE + QK_ROPE)), Q_LORA ** -0.5),
        "g_kv_norm": gain(ks[12], (DEPTH, KV_LORA)),
        "w_uk": nrm(ks[13], (DEPTH, KV_LORA, MLA_HEADS, QK_NOPE), KV_LORA ** -0.5),
        "w_uv": nrm(ks[14], (DEPTH, KV_LORA, MLA_HEADS, V_DIM), KV_LORA ** -0.5),
        "conv_w": nrm(ks[15], (DEPTH, CONV_W, CONV_DIM), CONV_W ** -0.5),
        "conv_b": nrm(ks[16], (DEPTH, CONV_DIM), 0.01),
        "dt_bias": dt_bias,
        "a_log": a_log,
        "d_skip": gain(ks[17], (DEPTH, SSD_HEADS)),
        "g_ssd_norm": gain(ks[18], (DEPTH, SSD_INNER)),
        "g_attn_out": gain(ks[21], (DEPTH, MLA_INNER)),
        "w_out": nrm(ks[22], (DEPTH, MIX_WIDTH, D_MODEL), MIX_WIDTH ** -0.5),
        "g_ffn_norm": gain(ks[23], (DEPTH, D_MODEL)),
        "w_router": nrm(ks[24], (DEPTH, D_MODEL, N_EXPERTS), D_MODEL ** -0.5),
        "b_router": nrm(ks[25], (DEPTH, N_EXPERTS), 0.01),
        "w_gate": nrm(ks[26], (DEPTH, N_EXPERTS, D_MODEL, D_FF), D_MODEL ** -0.5),
        "b_gate": nrm(ks[27], (DEPTH, N_EXPERTS, D_FF), 0.01),
        "w_up": nrm(ks[28], (DEPTH, N_EXPERTS, D_MODEL, D_FF), D_MODEL ** -0.5),
        "b_up": nrm(ks[29], (DEPTH, N_EXPERTS, D_FF), 0.01),
        "w_down": nrm(ks[30], (DEPTH, N_EXPERTS, D_FF, D_MODEL), D_FF ** -0.5),
        "b_down": nrm(ks[31], (DEPTH, N_EXPERTS, D_MODEL), 0.01),
        "g_final_norm": gain(ks[32], (D_MODEL,)),
    }


def reference(x_prompt, x_sample, cache_ckv, cache_kpe, state_ssm, state_conv, page_table, meta_tokens,
              g_mix_norm, w_in, g_q_norm, w_q_up, g_kv_norm, w_uk, w_uv, conv_w, conv_b, dt_bias, a_log,
              d_skip, g_ssd_norm, g_attn_out, w_out, g_ffn_norm, w_router, b_router, w_gate, b_gate,
              w_up, b_up, w_down, b_down, g_final_norm):
    bp, seq = x_prompt.shape[:2]
    bs, dseq = x_sample.shape[:2]
    past_len = page_table.shape[1] * cache_ckv.shape[2]
    hp = jnp.concatenate([jnp.broadcast_to(meta_tokens[None].astype(x_prompt.dtype), (bp, N_META, D_MODEL)), x_prompt], axis=1)
    hs = x_sample
    pos_p = jnp.arange(seq + N_META)
    pos_s = past_len + jnp.arange(dseq)
    seg_p = ((N_META, N_META), (seq, CHUNK))
    seg_s = ((dseq, dseq),)
    l_ckv_p, l_kpe_p, l_ssm_p, l_conv_p = [], [], [], []
    l_ckv_s, l_kpe_s, l_ssm_s, l_conv_s = [], [], [], []
    for li in range(DEPTH):
        ql, qp, ckv_p, kpe_p, z, xbc, dtr = mixer_inputs(hp, pos_p, g_mix_norm[li], w_in[li], g_q_norm[li], w_q_up[li], g_kv_norm[li], w_uk[li])
        o_lat = mla_attend_prompt(ql, qp, ckv_p, kpe_p)
        y_ssd, ssm_p, conv_p = ssd_mixer(z, xbc, dtr, jnp.zeros((bp, CONV_W - 1, CONV_DIM), hp.dtype),
                                         jnp.zeros((bp, SSD_HEADS, SSD_HEAD_DIM, D_STATE), hp.dtype), seg_p,
                                         conv_w[li], conv_b[li], dt_bias[li], a_log[li], d_skip[li], g_ssd_norm[li])
        hp = hp + mixer_merge(y_ssd, o_lat, w_uv[li], g_attn_out[li], w_out[li])
        ql, qp, ckv_s, kpe_s, z, xbc, dtr = mixer_inputs(hs, pos_s, g_mix_norm[li], w_in[li], g_q_norm[li], w_q_up[li], g_kv_norm[li], w_uk[li])
        o_lat = mla_attend_sample(ql, qp, ckv_s, kpe_s, cache_ckv[li], cache_kpe[li], page_table)
        y_ssd, ssm_s, conv_s = ssd_mixer(z, xbc, dtr, state_conv[li], state_ssm[li], seg_s,
                                         conv_w[li], conv_b[li], dt_bias[li], a_log[li], d_skip[li], g_ssd_norm[li])
        hs = hs + mixer_merge(y_ssd, o_lat, w_uv[li], g_attn_out[li], w_out[li])
        if li == DEPTH - 1:
            hp = hp[:, N_META:]
        rows_p = hp.shape[0] * hp.shape[1]
        rows = jnp.concatenate([hp.reshape(rows_p, D_MODEL), hs.reshape(bs * dseq, D_MODEL)], axis=0)
        f = moe(rmsnorm(rows, g_ffn_norm[li]), w_router[li], b_router[li], w_gate[li], b_gate[li],
                w_up[li], b_up[li], w_down[li], b_down[li])
        hp = hp + f[:rows_p].reshape(hp.shape)
        hs = hs + f[rows_p:].reshape(hs.shape)
        l_ckv_p.append(ckv_p); l_kpe_p.append(kpe_p); l_ssm_p.append(ssm_p); l_conv_p.append(conv_p)
        l_ckv_s.append(ckv_s); l_kpe_s.append(kpe_s); l_ssm_s.append(ssm_s); l_conv_s.append(conv_s)
    y_prompt = rmsnorm(hp, g_final_norm)
    y_sample = rmsnorm(hs, g_final_norm)
    ckv_prompt = jnp.stack(l_ckv_p)
    kpe_prompt = jnp.stack(l_kpe_p)
    ssm_prompt = jnp.stack(l_ssm_p)
    conv_prompt = jnp.stack(l_conv_p)
    ckv_sample = jnp.stack(l_ckv_s)
    kpe_sample = jnp.stack(l_kpe_s)
    ssm_sample = jnp.stack(l_ssm_s)
    conv_sample = jnp.stack(l_conv_s)
    return (y_prompt, y_sample, ckv_prompt, kpe_prompt, ssm_prompt, conv_prompt, ckv_sample, kpe_sample, ssm_sample, conv_sample)
```

```python
import functools
import math

import jax
import jax.numpy as jnp
from jax import lax
from jax.experimental import pallas as pl
from jax.experimental.pallas import tpu as pltpu

F32 = jnp.float32
BF16 = jnp.bfloat16

D_MODEL = 2048
N_META = 16
EPS = 1e-6
SSD_HEADS = 16
SSD_HEAD_DIM = 64
SSD_INNER = SSD_HEADS * SSD_HEAD_DIM
SSD_GROUPS = 2
SSD_HPG = SSD_HEADS // SSD_GROUPS
D_STATE = 128
CONV_W = 4
CONV_DIM = SSD_INNER + 2 * SSD_GROUPS * D_STATE
CHUNK = 128
MLA_HEADS = 8
Q_LORA = 512
KV_LORA = 512
QK_NOPE = 128
QK_ROPE = 64
V_DIM = 128
MLA_INNER = MLA_HEADS * V_DIM
ROPE_THETA = 10000.0
SM_SCALE = (QK_NOPE + QK_ROPE) ** -0.5
N_EXPERTS = 32
TOP_K = 4
D_FF = 2048
SWIGLU_LIMIT = 7.0
SWIGLU_ALPHA = 1.702

LANES = 128
SUBLANES = 8
ROW_TILES = D_MODEL // LANES
VMEM_LIMIT_BYTES = 56 * 1024 * 1024

U_QA = 0
U_KV = U_QA + Q_LORA
U_Z = U_KV + KV_LORA
U_XBC = U_Z + SSD_INNER
U_KPE = U_XBC + CONV_DIM
U_TAIL = U_KPE + 2 * QK_ROPE
U_DTX = U_TAIL + LANES
U_WIDTH = U_DTX + SSD_INNER
Q_NOPE_W = MLA_HEADS * QK_NOPE
Q_ROPE_W = MLA_HEADS * QK_ROPE
Q_UP_WIDTH = Q_NOPE_W + 2 * Q_ROPE_W


def _params(semantics):
    return pltpu.CompilerParams(dimension_semantics=semantics, vmem_limit_bytes=VMEM_LIMIT_BYTES)


def _const_spec(shape):
    nd = len(shape)
    return pl.BlockSpec(shape, lambda *_: (0,) * nd, pipeline_mode=pl.Buffered(1))


def _rms(x, g):
    return x * lax.rsqrt(jnp.mean(x * x, axis=-1, keepdims=True) + EPS) * g


def _tiled_rows_shape(rows):
    return jax.ShapeDtypeStruct((rows // SUBLANES, ROW_TILES, SUBLANES, LANES), F32)


def _tiled_rows_spec(tm, row_block):
    return pl.BlockSpec((tm // SUBLANES, ROW_TILES, SUBLANES, LANES), lambda *a: (row_block(*a), 0, 0, 0))


def _gatherable(a):
    return a.reshape(a.shape[0], ROW_TILES, SUBLANES, 1, LANES)


_row_view = _gatherable


def _tile_view(ref):
    return ref.reshape(ref.shape[0], ROW_TILES, SUBLANES, LANES)


def _row_landing_scratch(rows):
    return pltpu.VMEM((rows // SUBLANES, ROW_TILES, SUBLANES, 1, LANES), F32)


def _gather_row_copy(src, src_row, dst, dst_row, sem):
    split = lambda r: (lax.shift_right_logical(r, 3), lax.bitwise_and(r, SUBLANES - 1))
    s_rt, s_sl = split(src_row)
    d_rt, d_sl = split(dst_row)
    return pltpu.make_async_copy(src.at[s_rt, :, s_sl], dst.at[d_rt, :, d_sl], sem)


def _rot_half_cols(w):
    half = QK_ROPE // 2
    return jnp.concatenate([-w[..., half:], w[..., :half]], axis=-1)


def _inproj_kernel(x_ref, gmix_ref, win_ref, gq_ref, wq_ref, wuk_ref, gkv_ref, cs_q_ref, cs_k_ref,
                   qlat_ref, qpe_ref, ckv_ref, ckvb_ref, kpe_ref, kpeb_ref, z_ref, xbc_ref, tail_ref, dtx_ref):
    xn = _rms(x_ref[...], gmix_ref[...]).astype(BF16)
    u = jnp.dot(xn, win_ref[...], preferred_element_type=F32)
    z_ref[...] = u[:, U_Z:U_Z + SSD_INNER]
    xbc_ref[...] = u[:, U_XBC:U_XBC + CONV_DIM]
    tail_ref[...] = u[:, U_TAIL:U_TAIL + LANES]
    dtx_ref[...] = u[:, U_DTX:U_DTX + SSD_INNER]
    ckv = _rms(u[:, U_KV:U_KV + KV_LORA], gkv_ref[...])
    ckv_ref[...] = ckv
    ckvb_ref[...] = ckv.astype(BF16)
    kprod = u[:, U_KPE:U_KPE + 2 * QK_ROPE] * cs_k_ref[...]
    kpe = kprod[:, :QK_ROPE] + kprod[:, QK_ROPE:]
    kpe_ref[...] = kpe
    kpeb_ref[...] = kpe.astype(BF16)
    qn = _rms(u[:, U_QA:U_QA + Q_LORA], gq_ref[...]).astype(BF16)
    q = jnp.dot(qn, wq_ref[...], preferred_element_type=F32)
    cs_q = cs_q_ref[...]
    qpe = (q[:, Q_NOPE_W:Q_NOPE_W + Q_ROPE_W] * cs_q[:, :Q_ROPE_W]
           + q[:, Q_NOPE_W + Q_ROPE_W:] * cs_q[:, Q_ROPE_W:]).astype(BF16)
    for h in range(MLA_HEADS):
        qpe_ref[h] = qpe[:, h * QK_ROPE:(h + 1) * QK_ROPE]
        qn_h = q[:, h * QK_NOPE:(h + 1) * QK_NOPE].astype(BF16)
        qlat_ref[h] = jnp.dot(qn_h, wuk_ref[h], preferred_element_type=F32).astype(BF16)


def _inproj(x, cs_q, cs_k, wts, tm):
    rows = x.shape[0]
    pos_rows = cs_q.shape[0]
    assert rows % tm == 0 and pos_rows % tm == 0
    n_pos = pos_rows // tm
    row = lambda w: pl.BlockSpec((tm, w), lambda i: (i, 0))
    pos = lambda w: pl.BlockSpec((tm, w), lambda i: (i % n_pos, 0))
    head = lambda w: pl.BlockSpec((MLA_HEADS, tm, w), lambda i: (0, i, 0))
    sds = jax.ShapeDtypeStruct
    return pl.pallas_call(
        _inproj_kernel,
        grid=(rows // tm,),
        in_specs=[row(D_MODEL), _const_spec((1, D_MODEL)), _const_spec((D_MODEL, U_WIDTH)),
                  _const_spec((1, Q_LORA)), _const_spec((Q_LORA, Q_UP_WIDTH)),
                  _const_spec((MLA_HEADS, QK_NOPE, KV_LORA)), _const_spec((1, KV_LORA)),
                  pos(2 * Q_ROPE_W), pos(2 * QK_ROPE)],
        out_specs=[head(KV_LORA), head(QK_ROPE), row(KV_LORA), row(KV_LORA), row(QK_ROPE), row(QK_ROPE),
                   row(SSD_INNER), row(CONV_DIM), row(LANES), row(SSD_INNER)],
        out_shape=[sds((MLA_HEADS, rows, KV_LORA), BF16), sds((MLA_HEADS, rows, QK_ROPE), BF16),
                   sds((rows, KV_LORA), F32), sds((rows, KV_LORA), BF16),
                   sds((rows, QK_ROPE), F32), sds((rows, QK_ROPE), BF16),
                   sds((rows, SSD_INNER), F32), sds((rows, CONV_DIM), F32), sds((rows, LANES), F32),
                   sds((rows, SSD_INNER), F32)],
        compiler_params=_params(("arbitrary",)),
        name="inproj",
    )(x, wts["g_mix"], wts["w_in"], wts["g_q"], wts["w_q"], wts["w_uk"], wts["g_kv"], cs_q, cs_k)


def _rope_tables(pos):
    inv_freq = ROPE_THETA ** (-jnp.arange(0, QK_ROPE, 2, dtype=F32) / QK_ROPE)
    ang = pos.astype(F32)[:, None] * inv_freq[None, :]
    cos = jnp.tile(jnp.cos(ang), (1, 2))
    sin = jnp.tile(jnp.sin(ang), (1, 2))
    cs_k = jnp.concatenate([cos, sin], axis=-1)
    cs_q = jnp.concatenate([jnp.tile(cos, (1, MLA_HEADS)), jnp.tile(sin, (1, MLA_HEADS))], axis=-1)
    return cs_q, cs_k


def _prep_mixer_weights(g_mix_norm, w_in, g_q_norm, w_q_up, g_kv_norm, w_uk):
    s = (0, Q_LORA, Q_LORA + KV_LORA, Q_LORA + KV_LORA + QK_ROPE,
         Q_LORA + KV_LORA + QK_ROPE + SSD_INNER, Q_LORA + KV_LORA + QK_ROPE + SSD_INNER + CONV_DIM)
    w_qa, w_kv, w_kpe, w_z, w_xbc, w_dt = (w_in[:, s[i]:(s[i + 1] if i + 1 < len(s) else None)] for i in range(6))
    pad = jnp.zeros((D_MODEL, LANES - SSD_HEADS), w_in.dtype)
    w_in_r = jnp.concatenate([w_qa, w_kv, w_z, w_xbc, w_kpe, _rot_half_cols(w_kpe), w_dt, pad,
                              jnp.repeat(w_dt, SSD_HEAD_DIM, axis=1)], axis=1)
    wq = w_q_up.reshape(Q_LORA, MLA_HEADS, QK_NOPE + QK_ROPE)
    wq_nope = wq[:, :, :QK_NOPE].reshape(Q_LORA, Q_NOPE_W)
    wq_rope = wq[:, :, QK_NOPE:]
    w_q_r = jnp.concatenate([wq_nope, wq_rope.reshape(Q_LORA, Q_ROPE_W),
                             _rot_half_cols(wq_rope).reshape(Q_LORA, Q_ROPE_W)], axis=1)
    return {
        "g_mix": g_mix_norm.reshape(1, D_MODEL), "w_in": w_in_r.astype(BF16),
        "g_q": g_q_norm.reshape(1, Q_LORA), "w_q": w_q_r.astype(BF16),
        "w_uk": jnp.transpose(w_uk, (1, 2, 0)).astype(BF16),
        "g_kv": g_kv_norm.reshape(1, KV_LORA),
    }


def _nt_dot(a, b):
    return lax.dot_general(a, b, (((1,), (1,)), ((), ())), preferred_element_type=F32)


def _attn_out_proj(o_lat, wuv_ref, g_ref, rows):
    o_b = o_lat.astype(BF16)
    o = jnp.concatenate([jnp.dot(o_b[h * rows:(h + 1) * rows], wuv_ref[h], preferred_element_type=F32)
                         for h in range(MLA_HEADS)], axis=-1)
    return _rms(o, g_ref[...])


def _attn_prompt_kernel(qlat_ref, qpe_ref, ckv_ref, kpe_ref, ckvm_ref, kpem_ref, wuv_ref, g_ref,
                        o_ref, m_ref, l_ref, acc_ref, *, bq):
    qi = pl.program_id(1)
    rows = MLA_HEADS * bq
    q_lat = qlat_ref[...].reshape(rows, KV_LORA)
    q_pe = qpe_ref[...].reshape(rows, QK_ROPE)

    def scores(c, kp):
        return (_nt_dot(q_lat, c) + _nt_dot(q_pe, kp)) * SM_SCALE

    cm = ckvm_ref[...]
    s = scores(cm, kpem_ref[...])
    m0 = jnp.max(s, axis=-1, keepdims=True)
    p = jnp.exp(s - m0)
    m_ref[...] = m0
    l_ref[...] = jnp.sum(p, axis=-1, keepdims=True)
    acc_ref[...] = jnp.dot(p.astype(BF16), cm, preferred_element_type=F32)

    def absorb(j, masked):
        k0 = pl.multiple_of(j * bq, bq)
        c = ckv_ref[pl.ds(k0, bq), :]
        s = scores(c, kpe_ref[pl.ds(k0, bq), :])
        if masked:
            causal = (lax.broadcasted_iota(jnp.int32, (bq, bq), 0) >= lax.broadcasted_iota(jnp.int32, (bq, bq), 1))
            s = jnp.where(causal[None], s.reshape(MLA_HEADS, bq, bq), -jnp.inf).reshape(rows, bq)
        m_old = m_ref[...]
        m_new = jnp.maximum(m_old, jnp.max(s, axis=-1, keepdims=True))
        corr = jnp.exp(m_old - m_new)
        p = jnp.exp(s - m_new)
        m_ref[...] = m_new
        l_ref[...] = l_ref[...] * corr + jnp.sum(p, axis=-1, keepdims=True)
        acc_ref[...] = acc_ref[...] * corr + jnp.dot(p.astype(BF16), c, preferred_element_type=F32)

    def full_block(j, carry):
        absorb(j, False)
        return carry

    lax.fori_loop(0, qi, full_block, 0)
    absorb(qi, True)
    o_ref[...] = _attn_out_proj(acc_ref[...] / l_ref[...], wuv_ref, g_ref, bq).astype(o_ref.dtype)


def _attn_prompt(qlat, qpe, ckvb, kpeb, ckvb_meta, kpeb_meta, w_uv_r, g_attn, bq):
    _, nb, seq, _ = qlat.shape
    assert seq % bq == 0
    rows = MLA_HEADS * bq
    return pl.pallas_call(
        functools.partial(_attn_prompt_kernel, bq=bq),
        grid=(nb, seq // bq),
        in_specs=[pl.BlockSpec((MLA_HEADS, None, bq, KV_LORA), lambda b, i: (0, b, i, 0)),
                  pl.BlockSpec((MLA_HEADS, None, bq, QK_ROPE), lambda b, i: (0, b, i, 0)),
                  pl.BlockSpec((None, seq, KV_LORA), lambda b, i: (b, 0, 0)),
                  pl.BlockSpec((None, seq, QK_ROPE), lambda b, i: (b, 0, 0)),
                  _const_spec((N_META, KV_LORA)), _const_spec((N_META, QK_ROPE)),
                  _const_spec((MLA_HEADS, KV_LORA, V_DIM)), _const_spec((1, MLA_INNER))],
        out_specs=pl.BlockSpec((None, bq, MLA_INNER), lambda b, i: (b, i, 0)),
        out_shape=jax.ShapeDtypeStruct((nb, seq, MLA_INNER), BF16),
        scratch_shapes=[pltpu.VMEM((rows, 1), F32), pltpu.VMEM((rows, 1), F32), pltpu.VMEM((rows, KV_LORA), F32)],
        compiler_params=_params(("arbitrary", "arbitrary")),
        name="attn_prompt",
    )(qlat, qpe, ckvb, kpeb, ckvb_meta, kpeb_meta, w_uv_r, g_attn)


HIST_ROWS = 8
GROUP_W = SSD_INNER // SSD_GROUPS


def _silu(x):
    return x * jax.nn.sigmoid(x)


def _softplus(x):
    return jnp.maximum(x, 0.0) + jnp.log1p(jnp.exp(-jnp.abs(x)))


def _split3_dot(m_bf16, a):
    hi = a.astype(BF16)
    r1 = a - hi.astype(F32)
    mid = r1.astype(BF16)
    lo = (r1 - mid.astype(F32)).astype(BF16)
    dot = lambda t: jnp.dot(m_bf16, t, preferred_element_type=F32)
    return dot(hi) + dot(mid) + dot(lo)


def _conv_silu(xpad_ref, xbc, hist_rows_ref, cw_ref, cb_ref, first, lc, valid):
    h0 = HIST_ROWS - (CONV_W - 1)

    @pl.when(first)
    def _():
        xpad_ref[h0:HIST_ROWS, :] = hist_rows_ref[...]

    xpad_ref[HIST_ROWS:HIST_ROWS + lc, :] = xbc
    acc = cb_ref[...] + cw_ref[CONV_W - 1:CONV_W, :] * xbc
    for k in range(CONV_W - 1):
        acc = acc + cw_ref[k:k + 1, :] * xpad_ref[h0 + k:h0 + k + lc, :]
    tail = xpad_ref[h0 + valid:HIST_ROWS + valid, :]
    xpad_ref[h0:HIST_ROWS, :] = tail
    return _silu(acc), tail


def _gated_group_norm(y, z, g):
    yg = y * _silu(z)
    parts = []
    for gi in range(SSD_GROUPS):
        part = yg[:, gi * GROUP_W:(gi + 1) * GROUP_W]
        parts.append(part * lax.rsqrt(jnp.mean(part * part, axis=-1, keepdims=True) + EPS))
    return jnp.concatenate(parts, axis=-1) * g


def _ssd_chunk_kernel(z_ref, xbc_ref, dt16_ref, dtx_ref, hist_ref, s0_ref, cw_ref, cb_ref,
                      dtb16_ref, alog16_ref, dtbx_ref, alogx_ref, dskx_ref, g_ref,
                      y_ref, sout_ref, cout_ref, st_ref, xpad_ref, *, lc, valid):
    c = pl.program_id(1)

    @pl.when(c == 0)
    def _():
        st_ref[...] = s0_ref[...].T

    xc, tail = _conv_silu(xpad_ref, xbc_ref[...], hist_ref, cw_ref, cb_ref, c == 0, lc, valid)
    xs = xc[:, :SSD_INNER]
    row_i = lax.broadcasted_iota(jnp.int32, (lc, lc), 0)
    col_j = lax.broadcasted_iota(jnp.int32, (lc, lc), 1)
    causal = row_i >= col_j
    tril = jnp.where(causal, 1.0, 0.0).astype(BF16)

    def dt_and_cumsum(raw_ref, bias_ref, alog_ref):
        dt = _softplus(raw_ref[...] + bias_ref[...])
        if valid < lc:
            dt = jnp.where(lax.broadcasted_iota(jnp.int32, dt.shape, 0) < valid, dt, 0.0)
        return dt, _split3_dot(tril, dt * (-jnp.exp(alog_ref[...])))

    _, acs16 = dt_and_cumsum(dt16_ref, dtb16_ref, alog16_ref)
    dtx, acsx = dt_and_cumsum(dtx_ref, dtbx_ref, alogx_ref)
    acs16_t = acs16.T
    xdt = (xs * dtx).astype(BF16)
    lane = lax.broadcasted_iota(jnp.int32, (lc, 2 * SSD_HEAD_DIM), 1)
    low_half = lane < SSD_HEAD_DIM
    zero = jnp.zeros((lc, 2 * SSD_HEAD_DIM), BF16)
    st = st_ref[...]
    decay_out = jnp.exp(acsx[lc - 1:lc, :] - acsx)
    xw = (xs * decay_out * dtx).astype(BF16)
    y_parts, st_parts = [], []
    for gi in range(SSD_GROUPS):
        b0 = SSD_INNER + gi * D_STATE
        c0 = SSD_INNER + SSD_GROUPS * D_STATE + gi * D_STATE
        bm = xc[:, b0:b0 + D_STATE].astype(BF16)
        cm = xc[:, c0:c0 + D_STATE].astype(BF16)
        cb = _nt_dot(cm, bm)
        lanes = slice(gi * GROUP_W, (gi + 1) * GROUP_W)
        y_off = jnp.dot(cm, st[:, lanes].astype(BF16), preferred_element_type=F32) * jnp.exp(acsx[:, lanes])
        for pair in range(SSD_HPG // 2):
            h0 = gi * SSD_HPG + 2 * pair
            pl_lanes = slice(h0 * SSD_HEAD_DIM, (h0 + 2) * SSD_HEAD_DIM)
            xpair = xdt[:, pl_lanes]
            y_pair = None
            for t in range(2):
                h = h0 + t
                seg = acs16[:, h:h + 1] - acs16_t[h:h + 1, :]
                m = (cb * jnp.exp(jnp.where(causal, seg, -jnp.inf))).astype(BF16)
                x_h = jnp.where(low_half, xpair, zero) if t == 0 else jnp.where(low_half, zero, xpair)
                y_h = jnp.dot(m, x_h, preferred_element_type=F32)
                y_pair = y_h if y_pair is None else y_pair + y_h
            y_parts.append(y_pair + y_off[:, pl_lanes.start - lanes.start:pl_lanes.stop - lanes.start])
        inc = lax.dot_general(bm, xw[:, lanes], (((0,), (0,)), ((), ())), preferred_element_type=F32)
        st_parts.append(st[:, lanes] * jnp.exp(acsx[lc - 1:lc, lanes]) + inc)
    st_new = jnp.concatenate(st_parts, axis=-1)
    st_ref[...] = st_new
    y = jnp.concatenate(y_parts, axis=-1) + dskx_ref[...] * xs
    y_ref[...] = _gated_group_norm(y, z_ref[...], g_ref[...]).astype(y_ref.dtype)

    @pl.when(c == pl.num_programs(1) - 1)
    def _():
        sout_ref[...] = st_new.T
        cout_ref[...] = tail


def _ssd_chunked(z, xbc, dt16, dtx, hist, s0, wts, lc, valid, shared_init):
    nb, t, _ = z.shape
    assert t % lc == 0
    tok = lambda w: pl.BlockSpec((None, lc, w), lambda b, c: (b, c, 0))
    init_idx = (lambda b, c: (0, 0, 0)) if shared_init else (lambda b, c: (b, 0, 0))
    per_b = lambda r, w: pl.BlockSpec((None, r, w), lambda b, c: (b, 0, 0))
    vec = lambda w: _const_spec((1, w))
    sds = jax.ShapeDtypeStruct
    return pl.pallas_call(
        functools.partial(_ssd_chunk_kernel, lc=lc, valid=valid),
        grid=(nb, t // lc),
        in_specs=[tok(SSD_INNER), tok(CONV_DIM), tok(LANES), tok(SSD_INNER),
                  pl.BlockSpec((None, CONV_W - 1, CONV_DIM), init_idx),
                  pl.BlockSpec((None, SSD_INNER, D_STATE), init_idx),
                  _const_spec((CONV_W, CONV_DIM)), vec(CONV_DIM), vec(LANES), vec(LANES),
                  vec(SSD_INNER), vec(SSD_INNER), vec(SSD_INNER), vec(SSD_INNER)],
        out_specs=[tok(SSD_INNER), per_b(SSD_INNER, D_STATE), per_b(CONV_W - 1, CONV_DIM)],
        out_shape=[sds((nb, t, SSD_INNER), BF16), sds((nb, SSD_INNER, D_STATE), F32),
                   sds((nb, CONV_W - 1, CONV_DIM), F32)],
        scratch_shapes=[pltpu.VMEM((D_STATE, SSD_INNER), F32), pltpu.VMEM((HIST_ROWS + lc, CONV_DIM), F32)],
        compiler_params=_params(("arbitrary", "arbitrary")),
        name="ssd_chunked",
    )(z, xbc, dt16, dtx, hist, s0, wts["conv_w"], wts["conv_b"], wts["dtb16"], wts["alog16"],
      wts["dtbx"], wts["alogx"], wts["dskx"], wts["g_ssd"])


def _ssd_step_kernel(z_ref, xbc_ref, dtx_ref, hist_ref, s0_ref, cw_ref, cb_ref, dtbx_ref, alogx_ref, dskx_ref, g_ref,
                     y_ref, sout_ref, cout_ref, *, nseq, lc):
    hist_n = CONV_W - 1
    t_idx = lax.broadcasted_iota(jnp.int32, (lc, SSD_INNER), 0)
    t_idx_g = lax.broadcasted_iota(jnp.int32, (lc, GROUP_W), 0)
    a_neg = -jnp.exp(alogx_ref[...])
    for b in range(nseq):
        xbc = xbc_ref[b]
        xp = jnp.concatenate([hist_ref[b], xbc], axis=0)
        acc = cb_ref[...] + cw_ref[hist_n:CONV_W, :] * xbc
        for k in range(hist_n):
            acc = acc + cw_ref[k:k + 1, :] * xp[k:k + lc]
        xc = _silu(acc)
        cout_ref[b] = xp[lc:lc + hist_n]
        xs = xc[:, :SSD_INNER]
        dt = _softplus(dtx_ref[b] + dtbx_ref[...])
        a = dt * a_neg
        acs = jnp.zeros_like(a)
        for l in range(lc):
            acs = acs + jnp.where(t_idx >= l, a[l:l + 1, :], 0.0)
        xdt = xs * dt
        st = s0_ref[b].T
        decay_last = jnp.exp(acs[lc - 1:lc, :])
        xw = (xs * jnp.exp(acs[lc - 1:lc, :] - acs) * dt).astype(BF16)
        y_parts, st_parts = [], []
        for gi in range(SSD_GROUPS):
            b0 = SSD_INNER + gi * D_STATE
            c0 = SSD_INNER + SSD_GROUPS * D_STATE + gi * D_STATE
            bm = xc[:, b0:b0 + D_STATE]
            cm = xc[:, c0:c0 + D_STATE]
            lanes = slice(gi * GROUP_W, (gi + 1) * GROUP_W)
            acs_g = acs[:, lanes]
            y_g = jnp.dot(cm.astype(BF16), st[:, lanes].astype(BF16), preferred_element_type=F32) * jnp.exp(acs_g)
            for j in range(lc):
                cb_j = jnp.sum(cm * bm[j:j + 1, :], axis=-1, keepdims=True)
                decay = jnp.exp(jnp.where(t_idx_g >= j, acs_g - acs_g[j:j + 1, :], -jnp.inf))
                y_g = y_g + cb_j * decay * xdt[j:j + 1, lanes]
            y_parts.append(y_g)
            inc = lax.dot_general(bm.astype(BF16), xw[:, lanes], (((0,), (0,)), ((), ())),
                                  preferred_element_type=F32)
            st_parts.append(st[:, lanes] * decay_last[:, lanes] + inc)
        sout_ref[b] = jnp.concatenate(st_parts, axis=-1).T
        y = jnp.concatenate(y_parts, axis=-1) + dskx_ref[...] * xs
        y_ref[b] = _gated_group_norm(y, z_ref[b], g_ref[...]).astype(y_ref.dtype)


def _ssd_step(z, xbc, dtx, hist, s0, wts, nseq):
    nb, lc, _ = z.shape
    assert nb % nseq == 0 and lc >= CONV_W - 1
    blk = lambda r, w: pl.BlockSpec((nseq, r, w), lambda i: (i, 0, 0))
    vec = lambda w: _const_spec((1, w))
    sds = jax.ShapeDtypeStruct
    return pl.pallas_call(
        functools.partial(_ssd_step_kernel, nseq=nseq, lc=lc),
        grid=(nb // nseq,),
        in_specs=[blk(lc, SSD_INNER), blk(lc, CONV_DIM), blk(lc, SSD_INNER), blk(CONV_W - 1, CONV_DIM),
                  blk(SSD_INNER, D_STATE), _const_spec((CONV_W, CONV_DIM)), vec(CONV_DIM),
                  vec(SSD_INNER), vec(SSD_INNER), vec(SSD_INNER), vec(SSD_INNER)],
        out_specs=[blk(lc, SSD_INNER), blk(SSD_INNER, D_STATE), blk(CONV_W - 1, CONV_DIM)],
        out_shape=[sds((nb, lc, SSD_INNER), F32), sds((nb, SSD_INNER, D_STATE), F32),
                   sds((nb, CONV_W - 1, CONV_DIM), F32)],
        compiler_params=_params(("arbitrary",)),
        name="ssd_step",
    )(z, xbc, dtx, hist, s0, wts["conv_w"], wts["conv_b"], wts["dtbx"], wts["alogx"], wts["dskx"], wts["g_ssd"])


def _prep_ssd_weights(conv_w, conv_b, dt_bias, a_log, d_skip, g_ssd_norm):
    pad16 = lambda v: jnp.pad(v.reshape(1, SSD_HEADS), ((0, 0), (0, LANES - SSD_HEADS)))
    expand = lambda v: jnp.repeat(v.reshape(1, SSD_HEADS), SSD_HEAD_DIM, axis=1)
    return {"conv_w": conv_w, "conv_b": conv_b.reshape(1, CONV_DIM), "dtb16": pad16(dt_bias), "alog16": pad16(a_log),
            "dtbx": expand(dt_bias), "alogx": expand(a_log), "dskx": expand(d_skip),
            "g_ssd": g_ssd_norm.reshape(1, SSD_INNER)}


PAGES_PER_STEP = 16


def _attn_sample_kernel(pt_ref, qlat_ref, qpe_ref, cnew_ref, knew_ref, wuv_ref, g_ref, ckv_hbm, kpe_hbm,
                        o_ref, m_ref, l_ref, acc_ref, cbuf, kbuf, sems, *, lq, page):
    b = pl.program_id(0)
    j = pl.program_id(1)
    nj = pl.num_programs(1)
    step = b * nj + j
    n_steps = pl.num_programs(0) * nj
    q_lat = qlat_ref[...]
    q_pe = qpe_ref[...]

    def page_copies(stp, slot, fn):
        sb = stp // nj
        sj = stp - sb * nj
        for p in range(PAGES_PER_STEP):
            pg = pt_ref[sb, sj * PAGES_PER_STEP + p]
            rows = pl.ds(p * page, page)
            fn(pltpu.make_async_copy(ckv_hbm.at[pg], cbuf.at[slot, rows], sems.at[slot]))
            fn(pltpu.make_async_copy(kpe_hbm.at[pg], kbuf.at[slot, rows], sems.at[slot]))

    @pl.when(step == 0)
    def _():
        page_copies(0, 0, lambda c: c.start())

    @pl.when(step + 1 < n_steps)
    def _():
        page_copies(step + 1, (step + 1) % 2, lambda c: c.start())

    @pl.when(j == 0)
    def _():
        m_ref[...] = jnp.full(m_ref.shape, -jnp.inf, F32)
        l_ref[...] = jnp.zeros(l_ref.shape, F32)
        acc_ref[...] = jnp.zeros(acc_ref.shape, F32)

    def absorb(s, c):
        m_old = m_ref[...]
        m_new = jnp.maximum(m_old, jnp.max(s, axis=-1, keepdims=True))
        corr = jnp.exp(m_old - m_new)
        p = jnp.exp(s - m_new)
        m_ref[...] = m_new
        l_ref[...] = l_ref[...] * corr + jnp.sum(p, axis=-1, keepdims=True)
        acc_ref[...] = acc_ref[...] * corr + jnp.dot(p.astype(BF16), c, preferred_element_type=F32)

    def scores(c, kp):
        return (_nt_dot(q_lat, c) + _nt_dot(q_pe, kp)) * SM_SCALE

    slot = step % 2
    page_copies(step, slot, lambda c: c.wait())
    c = cbuf[slot].astype(BF16)
    absorb(scores(c, kbuf[slot].astype(BF16)), c)

    @pl.when(j == nj - 1)
    def _():
        rows = MLA_HEADS * lq
        c_new = cnew_ref[...].astype(BF16)
        s = scores(c_new, knew_ref[...].astype(BF16))
        q_idx = lax.broadcasted_iota(jnp.int32, (rows, lq), 0) % lq
        k_idx = lax.broadcasted_iota(jnp.int32, (rows, lq), 1)
        absorb(jnp.where(q_idx >= k_idx, s, -jnp.inf), c_new)
        o_ref[...] = _attn_out_proj(acc_ref[...] / l_ref[...], wuv_ref, g_ref, lq).astype(o_ref.dtype)


def _attn_sample(page_table, qlat, qpe, c_new, k_new, pool_ckv, pool_kpe, w_uv_r, g_attn):
    nb, rows, _ = qlat.shape
    lq = rows // MLA_HEADS
    n_pages = page_table.shape[1]
    page = pool_ckv.shape[1]
    assert n_pages % PAGES_PER_STEP == 0
    per_b = lambda r, w: pl.BlockSpec((None, r, w), lambda b, j, pt: (b, 0, 0))
    const = lambda shape: pl.BlockSpec(shape, lambda b, j, pt: (0,) * len(shape), pipeline_mode=pl.Buffered(1))
    keys = PAGES_PER_STEP * page
    return pl.pallas_call(
        functools.partial(_attn_sample_kernel, lq=lq, page=page),
        grid_spec=pltpu.PrefetchScalarGridSpec(
            num_scalar_prefetch=1,
            grid=(nb, n_pages // PAGES_PER_STEP),
            in_specs=[per_b(rows, KV_LORA), per_b(rows, QK_ROPE), per_b(lq, KV_LORA), per_b(lq, QK_ROPE),
                      const((MLA_HEADS, KV_LORA, V_DIM)), const((1, MLA_INNER)),
                      pl.BlockSpec(memory_space=pl.ANY), pl.BlockSpec(memory_space=pl.ANY)],
            out_specs=pl.BlockSpec((None, lq, MLA_INNER), lambda b, j, pt: (b, 0, 0)),
            scratch_shapes=[pltpu.VMEM((rows, 1), F32), pltpu.VMEM((rows, 1), F32), pltpu.VMEM((rows, KV_LORA), F32),
                            pltpu.VMEM((2, keys, KV_LORA), F32), pltpu.VMEM((2, keys, QK_ROPE), F32),
                            pltpu.SemaphoreType.DMA((2,))]),
        out_shape=jax.ShapeDtypeStruct((nb, lq, MLA_INNER), F32),
        compiler_params=_params(("arbitrary", "arbitrary")),
        name="attn_sample",
    )(page_table, qlat, qpe, c_new, k_new, w_uv_r, g_attn, pool_ckv, pool_kpe)


def _merge_router_kernel(ha_ref, ya_ref, oa_ref, hb_ref, yb_ref, ob_ref, wa_ref, wb_ref, gffn_ref, wr_ref, br_ref,
                         h2_ref, xn_ref, topi_ref, gate_ref, rank_ref, cnt_ref, carry_ref, *, tm, steps_a):
    i = pl.program_id(0)

    @pl.when(i == 0)
    def _():
        carry_ref[...] = jnp.zeros(carry_ref.shape, F32)

    first = i < steps_a
    pick = lambda a_ref, b_ref, dt: jnp.where(first, a_ref[...].astype(dt), b_ref[...].astype(dt))
    mix = (jnp.dot(pick(ya_ref, yb_ref, BF16), wa_ref[...], preferred_element_type=F32)
           + jnp.dot(pick(oa_ref, ob_ref, BF16), wb_ref[...], preferred_element_type=F32))
    h2 = pick(ha_ref, hb_ref, F32) + mix
    h2_ref[...] = h2
    xn32 = _rms(h2, gffn_ref[...])
    for s in range(D_MODEL // LANES):
        xn_ref[:, s] = xn32[:, s * LANES:(s + 1) * LANES].reshape(tm // SUBLANES, SUBLANES, LANES)
    xn = xn32.astype(BF16)
    lane = lax.broadcasted_iota(jnp.int32, (tm, LANES), 1)
    logits = jnp.dot(xn, wr_ref[...], preferred_element_type=F32) + br_ref[...]
    logits = jnp.where(lane < N_EXPERTS, logits, -jnp.inf)
    vals, idxs = [], []
    for _ in range(TOP_K):
        v = jnp.max(logits, axis=-1, keepdims=True)
        idx = jnp.min(jnp.where(logits == v, lane, LANES), axis=-1, keepdims=True)
        vals.append(v)
        idxs.append(idx)
        logits = jnp.where(lane == idx, -jnp.inf, logits)
    exps = [jnp.exp(v - vals[0]) for v in vals]
    denom = exps[0]
    for e in exps[1:]:
        denom = denom + e
    hit = [lane == idx for idx in idxs]
    onehot = jnp.zeros((tm, LANES), F32)
    for hk in hit:
        onehot = onehot + jnp.where(hk, 1.0, 0.0)
    r_i = lax.broadcasted_iota(jnp.int32, (tm, tm), 0)
    c_j = lax.broadcasted_iota(jnp.int32, (tm, tm), 1)
    strict_tril = jnp.where(r_i > c_j, 1.0, 0.0).astype(BF16)
    before = jnp.dot(strict_tril, onehot.astype(BF16), preferred_element_type=F32) + carry_ref[...]
    carry_ref[...] = carry_ref[...] + jnp.sum(onehot, axis=0, keepdims=True)
    topi = jnp.zeros((tm, LANES), jnp.int32)
    gate = jnp.zeros((tm, LANES), F32)
    rank = jnp.zeros((tm, LANES), jnp.int32)
    for k in range(TOP_K):
        rk = jnp.sum(jnp.where(hit[k], before, 0.0), axis=-1, keepdims=True).astype(jnp.int32)
        topi = jnp.where(lane == k, idxs[k], topi)
        gate = jnp.where(lane == k, exps[k] / denom, gate)
        rank = jnp.where(lane == k, rk, rank)
    topi_ref[...] = topi
    gate_ref[...] = gate
    rank_ref[...] = rank
    cnt_ref[...] = carry_ref[...]


def _merge_router(group_a, group_b, wts, tm):
    rows_a, rows_b = group_a[0].shape[0], group_b[0].shape[0]
    assert rows_a % tm == 0 and rows_b % tm == 0
    steps_a, steps_b = rows_a // tm, rows_b // tm
    rows = rows_a + rows_b
    spec_a = lambda w: pl.BlockSpec((tm, w), lambda i: (jnp.minimum(i, steps_a - 1), 0))
    spec_b = lambda w: pl.BlockSpec((tm, w), lambda i: (jnp.maximum(i - steps_a, 0), 0))
    row = lambda w: pl.BlockSpec((tm, w), lambda i: (i, 0))
    widths = (D_MODEL, SSD_INNER, MLA_INNER)
    sds = jax.ShapeDtypeStruct
    return pl.pallas_call(
        functools.partial(_merge_router_kernel, tm=tm, steps_a=steps_a),
        grid=(steps_a + steps_b,),
        in_specs=[spec_a(w) for w in widths] + [spec_b(w) for w in widths]
                 + [_const_spec((SSD_INNER, D_MODEL)), _const_spec((MLA_INNER, D_MODEL)), _const_spec((1, D_MODEL)),
                    _const_spec((D_MODEL, LANES)), _const_spec((1, LANES))],
        out_specs=[row(D_MODEL), _tiled_rows_spec(tm, lambda i: i), row(LANES), row(LANES), row(LANES),
                   pl.BlockSpec((1, LANES), lambda i: (0, 0))],
        out_shape=[sds((rows, D_MODEL), F32), _tiled_rows_shape(rows), sds((rows, LANES), jnp.int32),
                   sds((rows, LANES), F32), sds((rows, LANES), jnp.int32), sds((1, LANES), F32)],
        scratch_shapes=[pltpu.VMEM((1, LANES), F32)],
        compiler_params=_params(("arbitrary",)),
        name="merge_router",
    )(*group_a, *group_b, wts["w_out_a"], wts["w_out_b"], wts["g_ffn"], wts["w_router"], wts["b_router"])


MOE_TM = 1024
MOE_TS = 256
MOE_TF = 256
COMBINE_TM = 256


def _moe_kernel(te_ref, nv_ref, src_ref, xn_hbm, wg_ref, wu_ref, wd_ref, bg_ref, bu_ref, bd_ref, y_ref,
                xbf, wg_s, wu_s, wd_s, sem):
    del te_ref
    i = pl.program_id(0)
    f = pl.program_id(1)
    nv = nv_ref[i]
    n_sub = MOE_TM // MOE_TS
    sub_tiles = MOE_TS // SUBLANES

    def when_sub_valid(body):
        for s in range(n_sub):
            pl.when(s * MOE_TS < nv)(functools.partial(body, s))

    for s in range(n_sub):
        @pl.when((f == 0) & (s * MOE_TS >= nv))
        def _():
            y_ref[s * sub_tiles:(s + 1) * sub_tiles] = jnp.zeros((sub_tiles, ROW_TILES, SUBLANES, LANES), F32)

    @pl.when((nv > 0) & (f == 0))
    def _():
        land = _row_view(y_ref)
        n_rows = ((nv + MOE_TS - 1) // MOE_TS) * MOE_TS

        def copy(r):
            return _gather_row_copy(xn_hbm, src_ref[i * MOE_TM + r], land, r, sem)

        lax.fori_loop(0, n_rows, lambda r, c: (copy(r).start(), c)[1], 0)
        lax.fori_loop(0, n_rows, lambda r, c: (copy(r).wait(), c)[1], 0)

        def convert(s):
            for ct in range(ROW_TILES):
                piece = y_ref[s * sub_tiles:(s + 1) * sub_tiles, ct].reshape(MOE_TS, LANES)
                xbf[s * MOE_TS:(s + 1) * MOE_TS, ct * LANES:(ct + 1) * LANES] = piece.astype(BF16)

        when_sub_valid(convert)

    @pl.when(nv > 0)
    def _():
        wg_s[...] = wg_ref[...].astype(BF16)
        wu_s[...] = wu_ref[...].astype(BF16)
        wd_s[...] = wd_ref[...].astype(BF16)

        def expert_mlp(s):
            x = xbf[s * MOE_TS:(s + 1) * MOE_TS, :]
            g = jnp.minimum(jnp.dot(x, wg_s[...], preferred_element_type=F32) + bg_ref[...], SWIGLU_LIMIT)
            u = jnp.clip(jnp.dot(x, wu_s[...], preferred_element_type=F32) + bu_ref[...],
                         -SWIGLU_LIMIT, SWIGLU_LIMIT)
            hdn = ((u + 1.0) * (g * jax.nn.sigmoid(SWIGLU_ALPHA * g))).astype(BF16)
            part = jnp.dot(hdn, wd_s[...], preferred_element_type=F32)
            tiles = slice(s * sub_tiles, (s + 1) * sub_tiles)

            @pl.when(f == 0)
            def _():
                first = part + bd_ref[...]
                for ct in range(ROW_TILES):
                    y_ref[tiles, ct] = first[:, ct * LANES:(ct + 1) * LANES].reshape(sub_tiles, SUBLANES, LANES)

            @pl.when(f > 0)
            def _():
                for ct in range(ROW_TILES):
                    y_ref[tiles, ct] = y_ref[tiles, ct] + part[:, ct * LANES:(ct + 1) * LANES].reshape(
                        sub_tiles, SUBLANES, LANES)

        when_sub_valid(expert_mlp)


def _moe_experts(tile_expert, tile_nvalid, src_rows, xn_rows, w_gate, b_gate, w_up, b_up, w_down, b_down):
    n_tiles = src_rows.shape[0] // MOE_TM
    n_f = D_FF // MOE_TF
    fidx = lambda i, f, nv: jnp.where(nv[i] > 0, f, n_f - 1)
    wspec = lambda shape, imap: pl.BlockSpec(shape, lambda i, f, te, nv, src: imap(te[i], fidx(i, f, nv)))
    return pl.pallas_call(
        _moe_kernel,
        grid_spec=pltpu.PrefetchScalarGridSpec(
            num_scalar_prefetch=3, grid=(n_tiles, n_f),
            in_specs=[pl.BlockSpec(memory_space=pl.ANY),
                      wspec((None, D_MODEL, MOE_TF), lambda e, f: (e, 0, f)),
                      wspec((None, D_MODEL, MOE_TF), lambda e, f: (e, 0, f)),
                      wspec((None, MOE_TF, D_MODEL), lambda e, f: (e, f, 0)),
                      wspec((None, 1, MOE_TF), lambda e, f: (e, 0, f)),
                      wspec((None, 1, MOE_TF), lambda e, f: (e, 0, f)),
                      wspec((None, 1, D_MODEL), lambda e, f: (e, 0, 0))],
            out_specs=_tiled_rows_spec(MOE_TM, lambda i, f, te, nv, src: i),
            scratch_shapes=[pltpu.VMEM((MOE_TM, D_MODEL), BF16), pltpu.VMEM((D_MODEL, MOE_TF), BF16),
                            pltpu.VMEM((D_MODEL, MOE_TF), BF16), pltpu.VMEM((MOE_TF, D_MODEL), BF16),
                            pltpu.SemaphoreType.DMA(())]),
        out_shape=_tiled_rows_shape(n_tiles * MOE_TM),
        compiler_params=_params(("arbitrary", "arbitrary")),
        name="moe_experts",
    )(tile_expert, tile_nvalid, src_rows, _gatherable(xn_rows), w_gate, w_up, w_down,
      b_gate.reshape(N_EXPERTS, 1, D_FF), b_up.reshape(N_EXPERTS, 1, D_FF), b_down.reshape(N_EXPERTS, 1, D_MODEL))


def _combine_kernel(pos_ref, h2_ref, gate_ref, y_hbm, gfin_ref, out_ref, *scratch, tm, row_offset):
    bufs, sems = scratch[:2 * TOP_K], scratch[2 * TOP_K]
    i = pl.program_id(0)
    n = pl.num_programs(0)

    def copies(step, slot, fn):
        def body(t, carry):
            for k in range(TOP_K):
                fn(_gather_row_copy(y_hbm, pos_ref[(row_offset + step * tm + t) * TOP_K + k],
                                    bufs[slot * TOP_K + k], t, sems.at[slot]))
            return carry
        lax.fori_loop(0, tm, body, 0)

    def for_slot(slot_value, fn):
        for slot in range(2):
            pl.when(slot_value == slot)(functools.partial(fn, slot))

    @pl.when(i == 0)
    def _():
        copies(0, 0, lambda c: c.start())

    @pl.when(i + 1 < n)
    def _():
        for_slot((i + 1) % 2, lambda slot: copies(i + 1, slot, lambda c: c.start()))

    def finish(slot):
        copies(i, slot, lambda c: c.wait())
        gate = gate_ref[...]
        gates = [jnp.broadcast_to(gate[:, k:k + 1], (tm, LANES)) for k in range(TOP_K)]
        views = [_tile_view(bufs[slot * TOP_K + k]) for k in range(TOP_K)]
        parts = []
        for ct in range(ROW_TILES):
            acc = h2_ref[:, ct * LANES:(ct + 1) * LANES]
            for k in range(TOP_K):
                acc = acc + gates[k] * views[k][:, ct].reshape(tm, LANES)
            parts.append(acc)
        out_ref[...] = _rms(jnp.concatenate(parts, axis=-1), gfin_ref[...])

    for_slot(i % 2, finish)


def _combine(pos_flat, h2, gate, y_rows, g_final, row_offset, rows):
    tm = COMBINE_TM
    assert rows % tm == 0 and row_offset % tm == 0
    off = row_offset // tm
    return pl.pallas_call(
        functools.partial(_combine_kernel, tm=tm, row_offset=row_offset),
        grid_spec=pltpu.PrefetchScalarGridSpec(
            num_scalar_prefetch=1, grid=(rows // tm,),
            in_specs=[pl.BlockSpec((tm, D_MODEL), lambda i, pos: (i + off, 0)),
                      pl.BlockSpec((tm, LANES), lambda i, pos: (i + off, 0)),
                      pl.BlockSpec(memory_space=pl.ANY),
                      pl.BlockSpec((1, D_MODEL), lambda i, pos: (0, 0))],
            out_specs=pl.BlockSpec((tm, D_MODEL), lambda i, pos: (i, 0)),
            scratch_shapes=[_row_landing_scratch(tm) for _ in range(2 * TOP_K)] + [pltpu.SemaphoreType.DMA((2,))]),
        out_shape=jax.ShapeDtypeStruct((rows, D_MODEL), F32),
        compiler_params=_params(("arbitrary",)),
        name="moe_combine",
    )(pos_flat, h2, gate, _gatherable(y_rows), g_final)


def _routing_tables(topi, rank, counts, n_rows):
    counts = counts[0, :N_EXPERTS].astype(jnp.int32)
    tiles_per = (counts + MOE_TM - 1) // MOE_TM
    tile_end = jnp.cumsum(tiles_per)
    tile_start = tile_end - tiles_per
    n_tiles_max = (n_rows * TOP_K) // MOE_TM + N_EXPERTS
    pos = tile_start[topi] * MOE_TM + rank
    max_rows = n_tiles_max * MOE_TM
    src = jnp.zeros((max_rows,), jnp.int32).at[pos.reshape(-1)].set(
        jnp.repeat(jnp.arange(n_rows, dtype=jnp.int32), TOP_K))
    tile_id = jnp.arange(n_tiles_max, dtype=jnp.int32)
    n_used = tile_end[-1]
    last = jnp.maximum(n_used - 1, 0)
    tile_row = jnp.minimum(tile_id, last)
    tile_expert = jnp.minimum(jnp.searchsorted(tile_end, tile_row, side="right"), N_EXPERTS - 1).astype(jnp.int32)
    in_tile = tile_row - tile_start[tile_expert]
    nvalid = jnp.clip(counts[tile_expert] - in_tile * MOE_TM, 0, MOE_TM)
    nvalid = jnp.where(tile_id < n_used, nvalid, 0).astype(jnp.int32)
    return pos.reshape(-1).astype(jnp.int32), src, tile_expert, nvalid


def kernel(x_prompt, x_sample, cache_ckv, cache_kpe, state_ssm, state_conv, page_table, meta_tokens, g_mix_norm, w_in, g_q_norm, w_q_up, g_kv_norm, w_uk, w_uv, conv_w, conv_b, dt_bias, a_log, d_skip, g_ssd_norm, g_attn_out, w_out, g_ffn_norm, w_router, b_router, w_gate, b_gate, w_up, b_up, w_down, b_down, g_final_norm):
    assert w_in.shape[0] == 1, "single-layer model"
    bp, seq = x_prompt.shape[:2]
    bs, dseq = x_sample.shape[:2]
    rows_p, rows_s = bp * seq, bs * dseq
    past_len = page_table.shape[1] * cache_ckv.shape[2]
    wm = _prep_mixer_weights(g_mix_norm[0], w_in[0], g_q_norm[0], w_q_up[0], g_kv_norm[0], w_uk[0])
    ws = _prep_ssd_weights(conv_w[0], conv_b[0], dt_bias[0], a_log[0], d_skip[0], g_ssd_norm[0])
    w_uv_r = jnp.transpose(w_uv[0], (1, 0, 2)).astype(BF16)
    g_attn = g_attn_out[0].reshape(1, MLA_INNER)
    wr = {"w_out_a": w_out[0][:SSD_INNER].astype(BF16), "w_out_b": w_out[0][SSD_INNER:].astype(BF16),
          "g_ffn": g_ffn_norm[0].reshape(1, D_MODEL),
          "w_router": jnp.pad(w_router[0], ((0, 0), (0, LANES - N_EXPERTS))).astype(BF16),
          "b_router": jnp.pad(b_router[0], (0, LANES - N_EXPERTS)).reshape(1, LANES)}

    tm = 256
    a_m = _inproj(meta_tokens, *_rope_tables(jnp.arange(N_META)), wm, N_META)
    a_p = _inproj(x_prompt.reshape(rows_p, D_MODEL), *_rope_tables(N_META + jnp.arange(seq)), wm, tm)
    cs_s = [jnp.tile(t, (tm // dseq, 1)) for t in _rope_tables(past_len + jnp.arange(dseq))]
    a_s = _inproj(x_sample.reshape(rows_s, D_MODEL), *cs_s, wm, tm)
    qlat_m, qpe_m, ckv_m, ckvb_m, kpe_m, kpeb_m, z_m, xbc_m, dt16_m, dtx_m = a_m
    qlat_p, qpe_p, ckv_p, ckvb_p, kpe_p, kpeb_p, z_p, xbc_p, dt16_p, dtx_p = a_p
    qlat_s, qpe_s, ckv_s, ckvb_s, kpe_s, kpeb_s, z_s, xbc_s, dt16_s, dtx_s = a_s
    del qlat_m, qpe_m, ckvb_s, kpeb_s, dt16_s

    heads_first = lambda q, b, t: q.reshape(MLA_HEADS, b, t, q.shape[-1])
    o_p = _attn_prompt(heads_first(qlat_p, bp, seq), heads_first(qpe_p, bp, seq),
                       ckvb_p.reshape(bp, seq, KV_LORA), kpeb_p.reshape(bp, seq, QK_ROPE),
                       ckvb_m, kpeb_m, w_uv_r, g_attn, 256)
    per_seq = lambda q: jnp.transpose(heads_first(q, bs, dseq), (1, 0, 2, 3)).reshape(bs, MLA_HEADS * dseq, q.shape[-1])
    o_s = _attn_sample(page_table, per_seq(qlat_s), per_seq(qpe_s), ckv_s.reshape(bs, dseq, KV_LORA),
                       kpe_s.reshape(bs, dseq, QK_ROPE), cache_ckv[0], cache_kpe[0], w_uv_r, g_attn)

    pad_meta = lambda v: jnp.pad(v, ((0, CHUNK - N_META), (0, 0)))[None]
    _, ssm_m, conv_m = _ssd_chunked(pad_meta(z_m), pad_meta(xbc_m), pad_meta(dt16_m), pad_meta(dtx_m),
                                    jnp.zeros((1, CONV_W - 1, CONV_DIM), F32),
                                    jnp.zeros((1, SSD_INNER, D_STATE), F32), ws, CHUNK, N_META, True)
    per_b = lambda v, b, t: v.reshape(b, t, v.shape[-1])
    y_p, ssm_p, conv_p = _ssd_chunked(per_b(z_p, bp, seq), per_b(xbc_p, bp, seq), per_b(dt16_p, bp, seq),
                                      per_b(dtx_p, bp, seq), conv_m, ssm_m, ws, CHUNK, CHUNK, True)
    y_s, ssm_s, conv_s = _ssd_step(per_b(z_s, bs, dseq), per_b(xbc_s, bs, dseq), per_b(dtx_s, bs, dseq),
                                   state_conv[0], state_ssm[0].reshape(bs, SSD_INNER, D_STATE), ws, 8)

    total = rows_p + rows_s
    h2, xn_rows, topi, gate, rank, counts = _merge_router(
        (x_prompt.reshape(rows_p, D_MODEL), y_p.reshape(rows_p, SSD_INNER), o_p.reshape(rows_p, MLA_INNER)),
        (x_sample.reshape(rows_s, D_MODEL), y_s.reshape(rows_s, SSD_INNER), o_s.reshape(rows_s, MLA_INNER)),
        wr, tm)

    pos, src, tile_expert, tile_nvalid = _routing_tables(topi[:, :TOP_K], rank[:, :TOP_K], counts, total)
    y_rows = _moe_experts(tile_expert, tile_nvalid, src, xn_rows, w_gate[0], b_gate[0], w_up[0],
                          b_up[0], w_down[0], b_down[0])
    g_fin = g_final_norm.reshape(1, D_MODEL)
    y_prompt = _combine(pos, h2, gate, y_rows, g_fin, 0, rows_p).reshape(bp, seq, D_MODEL)
    y_sample = _combine(pos, h2, gate, y_rows, g_fin, rows_p, rows_s).reshape(bs, dseq, D_MODEL)

    with_meta = lambda m, p, w: jnp.concatenate(
        [jnp.broadcast_to(m[None], (bp, N_META, w)), p.reshape(bp, seq, w)], axis=1)[None]
    return (y_prompt, y_sample,
            with_meta(ckv_m, ckv_p, KV_LORA), with_meta(kpe_m, kpe_p, QK_ROPE),
            ssm_p.reshape(1, bp, SSD_HEADS, SSD_HEAD_DIM, D_STATE), conv_p[None],
            ckv_s.reshape(1, bs, dseq, KV_LORA), kpe_s.reshape(1, bs, dseq, QK_ROPE),
            ssm_s.reshape(1, bs, SSD_HEADS, SSD_HEAD_DIM, D_STATE), conv_s[None])
```

```python
import functools
import math

import jax
import jax.numpy as jnp
from jax import lax
from jax.experimental import pallas as pl
from jax.experimental.pallas import tpu as pltpu

F32 = jnp.float32
BF16 = jnp.bfloat16

D_MODEL = 2048
N_META = 16
EPS = 1e-6
SSD_HEADS = 16
SSD_HEAD_DIM = 64
SSD_INNER = SSD_HEADS * SSD_HEAD_DIM
SSD_GROUPS = 2
SSD_HPG = SSD_HEADS // SSD_GROUPS
D_STATE = 128
CONV_W = 4
CONV_DIM = SSD_INNER + 2 * SSD_GROUPS * D_STATE
CHUNK = 128
MLA_HEADS = 8
Q_LORA = 512
KV_LORA = 512
QK_NOPE = 128
QK_ROPE = 64
V_DIM = 128
MLA_INNER = MLA_HEADS * V_DIM
ROPE_THETA = 10000.0
SM_SCALE = (QK_NOPE + QK_ROPE) ** -0.5
N_EXPERTS = 32
TOP_K = 4
D_FF = 2048
SWIGLU_LIMIT = 7.0
SWIGLU_ALPHA = 1.702

LANES = 128
SUBLANES = 8
ROW_TILES = D_MODEL // LANES
VMEM_LIMIT_BYTES = 56 * 1024 * 1024

U_QA = 0
U_KV = U_QA + Q_LORA
U_Z = U_KV + KV_LORA
U_XBC = U_Z + SSD_INNER
U_KPE = U_XBC + CONV_DIM
U_TAIL = U_KPE + 2 * QK_ROPE
U_DTX = U_TAIL + LANES
U_WIDTH = U_DTX + SSD_INNER
Q_NOPE_W = MLA_HEADS * QK_NOPE
Q_ROPE_W = MLA_HEADS * QK_ROPE
Q_UP_WIDTH = Q_NOPE_W + 2 * Q_ROPE_W


def _params(semantics):
    return pltpu.CompilerParams(dimension_semantics=semantics, vmem_limit_bytes=VMEM_LIMIT_BYTES)


def _const_spec(shape):
    nd = len(shape)
    return pl.BlockSpec(shape, lambda *_: (0,) * nd, pipeline_mode=pl.Buffered(1))


def _rms(x, g):
    return x * lax.rsqrt(jnp.mean(x * x, axis=-1, keepdims=True) + EPS) * g


def _gatherable_rows_shape(rows):
    return jax.ShapeDtypeStruct((rows // SUBLANES, ROW_TILES, SUBLANES, 1, LANES), F32)


def _gatherable_rows_spec(tm, row_block):
    return pl.BlockSpec((tm // SUBLANES, ROW_TILES, SUBLANES, 1, LANES), lambda *a: (row_block(*a), 0, 0, 0, 0))


def _tile_view(ref):
    return ref.reshape(ref.shape[0], ROW_TILES, SUBLANES, LANES)


def _row_landing_scratch(rows):
    return pltpu.VMEM((rows // SUBLANES, ROW_TILES, SUBLANES, 1, LANES), F32)


def _gather_row_copy(src, src_row, dst, dst_row, sem):
    def split(r):
        if isinstance(r, int):
            return r // SUBLANES, r % SUBLANES
        return lax.shift_right_logical(r, 3), lax.bitwise_and(r, SUBLANES - 1)

    s_rt, s_sl = split(src_row)
    d_rt, d_sl = split(dst_row)
    return pltpu.make_async_copy(src.at[s_rt, :, s_sl], dst.at[d_rt, :, d_sl], sem)


def _rot_half_cols(w):
    half = QK_ROPE // 2
    return jnp.concatenate([-w[..., half:], w[..., :half]], axis=-1)


def _inproj_kernel(x_ref, gmix_ref, win_ref, gq_ref, wq_ref, wuk_ref, gkv_ref, cs_q_ref, cs_k_ref,
                   qlat_ref, qpe_ref, ckv_ref, ckvb_ref, kpe_ref, kpeb_ref, z_ref, xbc_ref, tail_ref, dtx_ref):
    xn = _rms(x_ref[...], gmix_ref[...]).astype(BF16)
    u = jnp.dot(xn, win_ref[...], preferred_element_type=F32)
    z_ref[...] = u[:, U_Z:U_Z + SSD_INNER]
    xbc_ref[...] = u[:, U_XBC:U_XBC + CONV_DIM]
    tail_ref[...] = u[:, U_TAIL:U_TAIL + LANES]
    dtx_ref[...] = u[:, U_DTX:U_DTX + SSD_INNER]
    ckv = _rms(u[:, U_KV:U_KV + KV_LORA], gkv_ref[...])
    ckv_ref[...] = ckv
    ckvb_ref[...] = ckv.astype(BF16)
    kprod = u[:, U_KPE:U_KPE + 2 * QK_ROPE] * cs_k_ref[...]
    kpe = kprod[:, :QK_ROPE] + kprod[:, QK_ROPE:]
    kpe_ref[...] = kpe
    kpeb_ref[...] = kpe.astype(BF16)
    qn = _rms(u[:, U_QA:U_QA + Q_LORA], gq_ref[...]).astype(BF16)
    q = jnp.dot(qn, wq_ref[...], preferred_element_type=F32)
    cs_q = cs_q_ref[...]
    qpe = (q[:, Q_NOPE_W:Q_NOPE_W + Q_ROPE_W] * cs_q[:, :Q_ROPE_W]
           + q[:, Q_NOPE_W + Q_ROPE_W:] * cs_q[:, Q_ROPE_W:]).astype(BF16)
    for h in range(MLA_HEADS):
        qpe_ref[h] = qpe[:, h * QK_ROPE:(h + 1) * QK_ROPE]
        qn_h = q[:, h * QK_NOPE:(h + 1) * QK_NOPE].astype(BF16)
        qlat_ref[h] = jnp.dot(qn_h, wuk_ref[h], preferred_element_type=F32).astype(BF16)


def _inproj(x, cs_q, cs_k, wts, tm):
    rows = x.shape[0]
    pos_rows = cs_q.shape[0]
    assert rows % tm == 0 and pos_rows % tm == 0
    n_pos = pos_rows // tm
    row = lambda w: pl.BlockSpec((tm, w), lambda i: (i, 0))
    pos = lambda w: pl.BlockSpec((tm, w), lambda i: (i % n_pos, 0))
    head = lambda w: pl.BlockSpec((MLA_HEADS, tm, w), lambda i: (0, i, 0))
    sds = jax.ShapeDtypeStruct
    return pl.pallas_call(
        _inproj_kernel,
        grid=(rows // tm,),
        in_specs=[row(D_MODEL), _const_spec((1, D_MODEL)), _const_spec((D_MODEL, U_WIDTH)),
                  _const_spec((1, Q_LORA)), _const_spec((Q_LORA, Q_UP_WIDTH)),
                  _const_spec((MLA_HEADS, QK_NOPE, KV_LORA)), _const_spec((1, KV_LORA)),
                  pos(2 * Q_ROPE_W), pos(2 * QK_ROPE)],
        out_specs=[head(KV_LORA), head(QK_ROPE), row(KV_LORA), row(KV_LORA), row(QK_ROPE), row(QK_ROPE),
                   row(SSD_INNER), row(CONV_DIM), row(LANES), row(SSD_INNER)],
        out_shape=[sds((MLA_HEADS, rows, KV_LORA), BF16), sds((MLA_HEADS, rows, QK_ROPE), BF16),
                   sds((rows, KV_LORA), F32), sds((rows, KV_LORA), BF16),
                   sds((rows, QK_ROPE), F32), sds((rows, QK_ROPE), BF16),
                   sds((rows, SSD_INNER), F32), sds((rows, CONV_DIM), F32), sds((rows, LANES), F32),
                   sds((rows, SSD_INNER), F32)],
        compiler_params=_params(("arbitrary",)),
        name="inproj",
    )(x, wts["g_mix"], wts["w_in"], wts["g_q"], wts["w_q"], wts["w_uk"], wts["g_kv"], cs_q, cs_k)


def _rope_tables(pos):
    inv_freq = ROPE_THETA ** (-jnp.arange(0, QK_ROPE, 2, dtype=F32) / QK_ROPE)
    ang = pos.astype(F32)[:, None] * inv_freq[None, :]
    cos = jnp.tile(jnp.cos(ang), (1, 2))
    sin = jnp.tile(jnp.sin(ang), (1, 2))
    cs_k = jnp.concatenate([cos, sin], axis=-1)
    cs_q = jnp.concatenate([jnp.tile(cos, (1, MLA_HEADS)), jnp.tile(sin, (1, MLA_HEADS))], axis=-1)
    return cs_q, cs_k


def _prep_mixer_weights(g_mix_norm, w_in, g_q_norm, w_q_up, g_kv_norm, w_uk):
    s = (0, Q_LORA, Q_LORA + KV_LORA, Q_LORA + KV_LORA + QK_ROPE,
         Q_LORA + KV_LORA + QK_ROPE + SSD_INNER, Q_LORA + KV_LORA + QK_ROPE + SSD_INNER + CONV_DIM)
    w_qa, w_kv, w_kpe, w_z, w_xbc, w_dt = (w_in[:, s[i]:(s[i + 1] if i + 1 < len(s) else None)] for i in range(6))
    pad = jnp.zeros((D_MODEL, LANES - SSD_HEADS), w_in.dtype)
    w_in_r = jnp.concatenate([w_qa, w_kv, w_z, w_xbc, w_kpe, _rot_half_cols(w_kpe), w_dt, pad,
                              jnp.repeat(w_dt, SSD_HEAD_DIM, axis=1)], axis=1)
    wq = w_q_up.reshape(Q_LORA, MLA_HEADS, QK_NOPE + QK_ROPE)
    wq_nope = wq[:, :, :QK_NOPE].reshape(Q_LORA, Q_NOPE_W)
    wq_rope = wq[:, :, QK_NOPE:]
    w_q_r = jnp.concatenate([wq_nope, wq_rope.reshape(Q_LORA, Q_ROPE_W),
                             _rot_half_cols(wq_rope).reshape(Q_LORA, Q_ROPE_W)], axis=1)
    return {
        "g_mix": g_mix_norm.reshape(1, D_MODEL), "w_in": w_in_r.astype(BF16),
        "g_q": g_q_norm.reshape(1, Q_LORA), "w_q": w_q_r.astype(BF16),
        "w_uk": jnp.transpose(w_uk, (1, 2, 0)).astype(BF16),
        "g_kv": g_kv_norm.reshape(1, KV_LORA),
    }


def _nt_dot(a, b):
    return lax.dot_general(a, b, (((1,), (1,)), ((), ())), preferred_element_type=F32)


def _attn_out_proj(o_lat, wuv_ref, g_ref, rows):
    o_b = o_lat.astype(BF16)
    o = jnp.concatenate([jnp.dot(o_b[h * rows:(h + 1) * rows], wuv_ref[h], preferred_element_type=F32)
                         for h in range(MLA_HEADS)], axis=-1)
    return _rms(o, g_ref[...])


def _attn_prompt_kernel(qlat_ref, qpe_ref, ckv_ref, kpe_ref, ckvm_ref, kpem_ref, wuv_ref, g_ref,
                        o_ref, m_ref, l_ref, acc_ref, *, bq):
    qi = pl.program_id(1)
    rows = MLA_HEADS * bq
    q_lat = qlat_ref[...].reshape(rows, KV_LORA)
    q_pe = qpe_ref[...].reshape(rows, QK_ROPE)

    def scores(c, kp):
        return (_nt_dot(q_lat, c) + _nt_dot(q_pe, kp)) * SM_SCALE

    cm = ckvm_ref[...]
    s = scores(cm, kpem_ref[...])
    m0 = jnp.max(s, axis=-1, keepdims=True)
    p = jnp.exp(s - m0)
    m_ref[...] = m0
    l_ref[...] = jnp.sum(p, axis=-1, keepdims=True)
    acc_ref[...] = jnp.dot(p.astype(BF16), cm, preferred_element_type=F32)

    def absorb(j, masked):
        k0 = pl.multiple_of(j * bq, bq)
        c = ckv_ref[pl.ds(k0, bq), :]
        s = scores(c, kpe_ref[pl.ds(k0, bq), :])
        if masked:
            causal = (lax.broadcasted_iota(jnp.int32, (bq, bq), 0) >= lax.broadcasted_iota(jnp.int32, (bq, bq), 1))
            s = jnp.where(causal[None], s.reshape(MLA_HEADS, bq, bq), -jnp.inf).reshape(rows, bq)
        m_old = m_ref[...]
        m_new = jnp.maximum(m_old, jnp.max(s, axis=-1, keepdims=True))
        corr = jnp.exp(m_old - m_new)
        p = jnp.exp(s - m_new)
        m_ref[...] = m_new
        l_ref[...] = l_ref[...] * corr + jnp.sum(p, axis=-1, keepdims=True)
        acc_ref[...] = acc_ref[...] * corr + jnp.dot(p.astype(BF16), c, preferred_element_type=F32)

    def full_block(j, carry):
        absorb(j, False)
        return carry

    lax.fori_loop(0, qi, full_block, 0)
    absorb(qi, True)
    o_ref[...] = _attn_out_proj(acc_ref[...] / l_ref[...], wuv_ref, g_ref, bq).astype(o_ref.dtype)


def _attn_prompt(qlat, qpe, ckvb, kpeb, ckvb_meta, kpeb_meta, w_uv_r, g_attn, bq):
    _, nb, seq, _ = qlat.shape
    assert seq % bq == 0
    rows = MLA_HEADS * bq
    return pl.pallas_call(
        functools.partial(_attn_prompt_kernel, bq=bq),
        grid=(nb, seq // bq),
        in_specs=[pl.BlockSpec((MLA_HEADS, None, bq, KV_LORA), lambda b, i: (0, b, i, 0)),
                  pl.BlockSpec((MLA_HEADS, None, bq, QK_ROPE), lambda b, i: (0, b, i, 0)),
                  pl.BlockSpec((None, seq, KV_LORA), lambda b, i: (b, 0, 0)),
                  pl.BlockSpec((None, seq, QK_ROPE), lambda b, i: (b, 0, 0)),
                  _const_spec((N_META, KV_LORA)), _const_spec((N_META, QK_ROPE)),
                  _const_spec((MLA_HEADS, KV_LORA, V_DIM)), _const_spec((1, MLA_INNER))],
        out_specs=pl.BlockSpec((None, bq, MLA_INNER), lambda b, i: (b, i, 0)),
        out_shape=jax.ShapeDtypeStruct((nb, seq, MLA_INNER), BF16),
        scratch_shapes=[pltpu.VMEM((rows, 1), F32), pltpu.VMEM((rows, 1), F32), pltpu.VMEM((rows, KV_LORA), F32)],
        compiler_params=_params(("arbitrary", "arbitrary")),
        name="attn_prompt",
    )(qlat, qpe, ckvb, kpeb, ckvb_meta, kpeb_meta, w_uv_r, g_attn)


HIST_ROWS = 8
GROUP_W = SSD_INNER // SSD_GROUPS


def _silu(x):
    return x * jax.nn.sigmoid(x)


def _softplus(x):
    return jnp.maximum(x, 0.0) + jnp.log1p(jnp.exp(-jnp.abs(x)))


def _split3_dot(m_bf16, a):
    hi = a.astype(BF16)
    r1 = a - hi.astype(F32)
    mid = r1.astype(BF16)
    lo = (r1 - mid.astype(F32)).astype(BF16)
    dot = lambda t: jnp.dot(m_bf16, t, preferred_element_type=F32)
    return dot(hi) + dot(mid) + dot(lo)


def _conv_silu(xpad_ref, xbc, hist_rows_ref, cw_ref, cb_ref, first, lc, valid):
    h0 = HIST_ROWS - (CONV_W - 1)

    @pl.when(first)
    def _():
        xpad_ref[h0:HIST_ROWS, :] = hist_rows_ref[...]

    xpad_ref[HIST_ROWS:HIST_ROWS + lc, :] = xbc
    acc = cb_ref[...] + cw_ref[CONV_W - 1:CONV_W, :] * xbc
    for k in range(CONV_W - 1):
        acc = acc + cw_ref[k:k + 1, :] * xpad_ref[h0 + k:h0 + k + lc, :]
    tail = xpad_ref[h0 + valid:HIST_ROWS + valid, :]
    xpad_ref[h0:HIST_ROWS, :] = tail
    return _silu(acc), tail


def _gated_group_norm(y, z, g):
    yg = y * _silu(z)
    parts = []
    for gi in range(SSD_GROUPS):
        part = yg[:, gi * GROUP_W:(gi + 1) * GROUP_W]
        parts.append(part * lax.rsqrt(jnp.mean(part * part, axis=-1, keepdims=True) + EPS))
    return jnp.concatenate(parts, axis=-1) * g


def _ssd_chunk_kernel(z_ref, xbc_ref, dt16_ref, dtx_ref, hist_ref, s0_ref, cw_ref, cb_ref,
                      dtb16_ref, alog16_ref, dtbx_ref, alogx_ref, dskx_ref, g_ref,
                      y_ref, sout_ref, cout_ref, st_ref, xpad_ref, *, lc, valid):
    c = pl.program_id(1)

    @pl.when(c == 0)
    def _():
        st_ref[...] = s0_ref[...].T

    xc, tail = _conv_silu(xpad_ref, xbc_ref[...], hist_ref, cw_ref, cb_ref, c == 0, lc, valid)
    xs = xc[:, :SSD_INNER]
    row_i = lax.broadcasted_iota(jnp.int32, (lc, lc), 0)
    col_j = lax.broadcasted_iota(jnp.int32, (lc, lc), 1)
    causal = row_i >= col_j
    tril = jnp.where(causal, 1.0, 0.0).astype(BF16)

    def dt_and_cumsum(raw_ref, bias_ref, alog_ref):
        dt = _softplus(raw_ref[...] + bias_ref[...])
        if valid < lc:
            dt = jnp.where(lax.broadcasted_iota(jnp.int32, dt.shape, 0) < valid, dt, 0.0)
        return dt, _split3_dot(tril, dt * (-jnp.exp(alog_ref[...])))

    _, acs16 = dt_and_cumsum(dt16_ref, dtb16_ref, alog16_ref)
    dtx, acsx = dt_and_cumsum(dtx_ref, dtbx_ref, alogx_ref)
    acs16_t = acs16.T
    xdt = (xs * dtx).astype(BF16)
    lane = lax.broadcasted_iota(jnp.int32, (lc, 2 * SSD_HEAD_DIM), 1)
    low_half = lane < SSD_HEAD_DIM
    zero = jnp.zeros((lc, 2 * SSD_HEAD_DIM), BF16)
    st = st_ref[...]
    decay_out = jnp.exp(acsx[lc - 1:lc, :] - acsx)
    xw = (xs * decay_out * dtx).astype(BF16)
    y_parts, st_parts = [], []
    for gi in range(SSD_GROUPS):
        b0 = SSD_INNER + gi * D_STATE
        c0 = SSD_INNER + SSD_GROUPS * D_STATE + gi * D_STATE
        bm = xc[:, b0:b0 + D_STATE].astype(BF16)
        cm = xc[:, c0:c0 + D_STATE].astype(BF16)
        cb = _nt_dot(cm, bm)
        lanes = slice(gi * GROUP_W, (gi + 1) * GROUP_W)
        y_off = jnp.dot(cm, st[:, lanes].astype(BF16), preferred_element_type=F32) * jnp.exp(acsx[:, lanes])
        for pair in range(SSD_HPG // 2):
            h0 = gi * SSD_HPG + 2 * pair
            pl_lanes = slice(h0 * SSD_HEAD_DIM, (h0 + 2) * SSD_HEAD_DIM)
            xpair = xdt[:, pl_lanes]
            y_pair = None
            for t in range(2):
                h = h0 + t
                seg = acs16[:, h:h + 1] - acs16_t[h:h + 1, :]
                m = (cb * jnp.exp(jnp.where(causal, seg, -jnp.inf))).astype(BF16)
                x_h = jnp.where(low_half, xpair, zero) if t == 0 else jnp.where(low_half, zero, xpair)
                y_h = jnp.dot(m, x_h, preferred_element_type=F32)
                y_pair = y_h if y_pair is None else y_pair + y_h
            y_parts.append(y_pair + y_off[:, pl_lanes.start - lanes.start:pl_lanes.stop - lanes.start])
        inc = lax.dot_general(bm, xw[:, lanes], (((0,), (0,)), ((), ())), preferred_element_type=F32)
        st_parts.append(st[:, lanes] * jnp.exp(acsx[lc - 1:lc, lanes]) + inc)
    st_new = jnp.concatenate(st_parts, axis=-1)
    st_ref[...] = st_new
    y = jnp.concatenate(y_parts, axis=-1) + dskx_ref[...] * xs
    y_ref[...] = _gated_group_norm(y, z_ref[...], g_ref[...]).astype(y_ref.dtype)

    @pl.when(c == pl.num_programs(1) - 1)
    def _():
        sout_ref[...] = st_new.T
        cout_ref[...] = tail


def _ssd_chunked(z, xbc, dt16, dtx, hist, s0, wts, lc, valid, shared_init):
    nb, t, _ = z.shape
    assert t % lc == 0
    tok = lambda w: pl.BlockSpec((None, lc, w), lambda b, c: (b, c, 0))
    init_idx = (lambda b, c: (0, 0, 0)) if shared_init else (lambda b, c: (b, 0, 0))
    per_b = lambda r, w: pl.BlockSpec((None, r, w), lambda b, c: (b, 0, 0))
    vec = lambda w: _const_spec((1, w))
    sds = jax.ShapeDtypeStruct
    return pl.pallas_call(
        functools.partial(_ssd_chunk_kernel, lc=lc, valid=valid),
        grid=(nb, t // lc),
        in_specs=[tok(SSD_INNER), tok(CONV_DIM), tok(LANES), tok(SSD_INNER),
                  pl.BlockSpec((None, CONV_W - 1, CONV_DIM), init_idx),
                  pl.BlockSpec((None, SSD_INNER, D_STATE), init_idx),
                  _const_spec((CONV_W, CONV_DIM)), vec(CONV_DIM), vec(LANES), vec(LANES),
                  vec(SSD_INNER), vec(SSD_INNER), vec(SSD_INNER), vec(SSD_INNER)],
        out_specs=[tok(SSD_INNER), per_b(SSD_INNER, D_STATE), per_b(CONV_W - 1, CONV_DIM)],
        out_shape=[sds((nb, t, SSD_INNER), BF16), sds((nb, SSD_INNER, D_STATE), F32),
                   sds((nb, CONV_W - 1, CONV_DIM), F32)],
        scratch_shapes=[pltpu.VMEM((D_STATE, SSD_INNER), F32), pltpu.VMEM((HIST_ROWS + lc, CONV_DIM), F32)],
        compiler_params=_params(("arbitrary", "arbitrary")),
        name="ssd_chunked",
    )(z, xbc, dt16, dtx, hist, s0, wts["conv_w"], wts["conv_b"], wts["dtb16"], wts["alog16"],
      wts["dtbx"], wts["alogx"], wts["dskx"], wts["g_ssd"])


def _ssd_step_kernel(z_ref, xbc_ref, dtx_ref, hist_ref, s0_ref, cw_ref, cb_ref, dtbx_ref, alogx_ref, dskx_ref, g_ref,
                     y_ref, sout_ref, cout_ref, *, nseq, lc):
    hist_n = CONV_W - 1
    t_idx = lax.broadcasted_iota(jnp.int32, (lc, SSD_INNER), 0)
    t_idx_g = lax.broadcasted_iota(jnp.int32, (lc, GROUP_W), 0)
    a_neg = -jnp.exp(alogx_ref[...])
    for b in range(nseq):
        xbc = xbc_ref[b]
        xp = jnp.concatenate([hist_ref[b], xbc], axis=0)
        acc = cb_ref[...] + cw_ref[hist_n:CONV_W, :] * xbc
        for k in range(hist_n):
            acc = acc + cw_ref[k:k + 1, :] * xp[k:k + lc]
        xc = _silu(acc)
        cout_ref[b] = xp[lc:lc + hist_n]
        xs = xc[:, :SSD_INNER]
        dt = _softplus(dtx_ref[b] + dtbx_ref[...])
        a = dt * a_neg
        acs = jnp.zeros_like(a)
        for l in range(lc):
            acs = acs + jnp.where(t_idx >= l, a[l:l + 1, :], 0.0)
        xdt = xs * dt
        st = s0_ref[b].T
        decay_last = jnp.exp(acs[lc - 1:lc, :])
        xw = (xs * jnp.exp(acs[lc - 1:lc, :] - acs) * dt).astype(BF16)
        y_parts, st_parts = [], []
        for gi in range(SSD_GROUPS):
            b0 = SSD_INNER + gi * D_STATE
            c0 = SSD_INNER + SSD_GROUPS * D_STATE + gi * D_STATE
            bm = xc[:, b0:b0 + D_STATE]
            cm = xc[:, c0:c0 + D_STATE]
            lanes = slice(gi * GROUP_W, (gi + 1) * GROUP_W)
            acs_g = acs[:, lanes]
            y_g = jnp.dot(cm.astype(BF16), st[:, lanes].astype(BF16), preferred_element_type=F32) * jnp.exp(acs_g)
            for j in range(lc):
                cb_j = jnp.sum(cm * bm[j:j + 1, :], axis=-1, keepdims=True)
                decay = jnp.exp(jnp.where(t_idx_g >= j, acs_g - acs_g[j:j + 1, :], -jnp.inf))
                y_g = y_g + cb_j * decay * xdt[j:j + 1, lanes]
            y_parts.append(y_g)
            inc = lax.dot_general(bm.astype(BF16), xw[:, lanes], (((0,), (0,)), ((), ())),
                                  preferred_element_type=F32)
            st_parts.append(st[:, lanes] * decay_last[:, lanes] + inc)
        sout_ref[b] = jnp.concatenate(st_parts, axis=-1).T
        y = jnp.concatenate(y_parts, axis=-1) + dskx_ref[...] * xs
        y_ref[b] = _gated_group_norm(y, z_ref[b], g_ref[...]).astype(y_ref.dtype)


def _ssd_step(z, xbc, dtx, hist, s0, wts, nseq):
    nb, lc, _ = z.shape
    assert nb % nseq == 0 and lc >= CONV_W - 1
    blk = lambda r, w: pl.BlockSpec((nseq, r, w), lambda i: (i, 0, 0))
    vec = lambda w: _const_spec((1, w))
    sds = jax.ShapeDtypeStruct
    return pl.pallas_call(
        functools.partial(_ssd_step_kernel, nseq=nseq, lc=lc),
        grid=(nb // nseq,),
        in_specs=[blk(lc, SSD_INNER), blk(lc, CONV_DIM), blk(lc, SSD_INNER), blk(CONV_W - 1, CONV_DIM),
                  blk(SSD_INNER, D_STATE), _const_spec((CONV_W, CONV_DIM)), vec(CONV_DIM),
                  vec(SSD_INNER), vec(SSD_INNER), vec(SSD_INNER), vec(SSD_INNER)],
        out_specs=[blk(lc, SSD_INNER), blk(SSD_INNER, D_STATE), blk(CONV_W - 1, CONV_DIM)],
        out_shape=[sds((nb, lc, SSD_INNER), F32), sds((nb, SSD_INNER, D_STATE), F32),
                   sds((nb, CONV_W - 1, CONV_DIM), F32)],
        compiler_params=_params(("arbitrary",)),
        name="ssd_step",
    )(z, xbc, dtx, hist, s0, wts["conv_w"], wts["conv_b"], wts["dtbx"], wts["alogx"], wts["dskx"], wts["g_ssd"])


def _prep_ssd_weights(conv_w, conv_b, dt_bias, a_log, d_skip, g_ssd_norm):
    pad16 = lambda v: jnp.pad(v.reshape(1, SSD_HEADS), ((0, 0), (0, LANES - SSD_HEADS)))
    expand = lambda v: jnp.repeat(v.reshape(1, SSD_HEADS), SSD_HEAD_DIM, axis=1)
    return {"conv_w": conv_w, "conv_b": conv_b.reshape(1, CONV_DIM), "dtb16": pad16(dt_bias), "alog16": pad16(a_log),
            "dtbx": expand(dt_bias), "alogx": expand(a_log), "dskx": expand(d_skip),
            "g_ssd": g_ssd_norm.reshape(1, SSD_INNER)}


PAGES_PER_STEP = 32


def _attn_sample_kernel(pt_ref, qlat_ref, qpe_ref, cnew_ref, knew_ref, wuv_ref, g_ref, ckv_hbm, kpe_hbm,
                        o_ref, m_ref, l_ref, acc_ref, cbuf, kbuf, sems, *, lq, page):
    b = pl.program_id(0)
    j = pl.program_id(1)
    nj = pl.num_programs(1)
    step = b * nj + j
    n_steps = pl.num_programs(0) * nj
    q_lat = qlat_ref[...]
    q_pe = qpe_ref[...]

    def page_copies(stp, slot, fn):
        sb = stp // nj
        sj = stp - sb * nj
        for p in range(PAGES_PER_STEP):
            pg = pt_ref[sb, sj * PAGES_PER_STEP + p]
            rows = pl.ds(p * page, page)
            fn(pltpu.make_async_copy(ckv_hbm.at[pg], cbuf.at[slot, rows], sems.at[slot]), p)
            fn(pltpu.make_async_copy(kpe_hbm.at[pg], kbuf.at[slot, rows], sems.at[slot]), p)

    start = lambda c, p: c.start(priority=p % 2)

    @pl.when(step == 0)
    def _():
        page_copies(0, 0, start)

    @pl.when(step + 1 < n_steps)
    def _():
        page_copies(step + 1, (step + 1) % 2, start)

    @pl.when(j == 0)
    def _():
        m_ref[...] = jnp.full(m_ref.shape, -jnp.inf, F32)
        l_ref[...] = jnp.zeros(l_ref.shape, F32)
        acc_ref[...] = jnp.zeros(acc_ref.shape, F32)

    def absorb(s, c):
        m_old = m_ref[...]
        m_new = jnp.maximum(m_old, jnp.max(s, axis=-1, keepdims=True))
        corr = jnp.exp(m_old - m_new)
        p = jnp.exp(s - m_new)
        m_ref[...] = m_new
        l_ref[...] = l_ref[...] * corr + jnp.sum(p, axis=-1, keepdims=True)
        acc_ref[...] = acc_ref[...] * corr + jnp.dot(p.astype(BF16), c, preferred_element_type=F32)

    def scores(c, kp):
        return (_nt_dot(q_lat, c) + _nt_dot(q_pe, kp)) * SM_SCALE

    slot = step % 2
    page_copies(step, slot, lambda c, p: c.wait())
    c = cbuf[slot].astype(BF16)
    absorb(scores(c, kbuf[slot].astype(BF16)), c)

    @pl.when(j == nj - 1)
    def _():
        rows = MLA_HEADS * lq
        c_new = cnew_ref[...].astype(BF16)
        s = scores(c_new, knew_ref[...].astype(BF16))
        q_idx = lax.broadcasted_iota(jnp.int32, (rows, lq), 0) % lq
        k_idx = lax.broadcasted_iota(jnp.int32, (rows, lq), 1)
        absorb(jnp.where(q_idx >= k_idx, s, -jnp.inf), c_new)
        o_ref[...] = _attn_out_proj(acc_ref[...] / l_ref[...], wuv_ref, g_ref, lq).astype(o_ref.dtype)


def _attn_sample(page_table, qlat, qpe, c_new, k_new, pool_ckv, pool_kpe, w_uv_r, g_attn):
    nb, rows, _ = qlat.shape
    lq = rows // MLA_HEADS
    n_pages = page_table.shape[1]
    page = pool_ckv.shape[1]
    assert n_pages % PAGES_PER_STEP == 0
    per_b = lambda r, w: pl.BlockSpec((None, r, w), lambda b, j, pt: (b, 0, 0))
    const = lambda shape: pl.BlockSpec(shape, lambda b, j, pt: (0,) * len(shape), pipeline_mode=pl.Buffered(1))
    keys = PAGES_PER_STEP * page
    return pl.pallas_call(
        functools.partial(_attn_sample_kernel, lq=lq, page=page),
        grid_spec=pltpu.PrefetchScalarGridSpec(
            num_scalar_prefetch=1,
            grid=(nb, n_pages // PAGES_PER_STEP),
            in_specs=[per_b(rows, KV_LORA), per_b(rows, QK_ROPE), per_b(lq, KV_LORA), per_b(lq, QK_ROPE),
                      const((MLA_HEADS, KV_LORA, V_DIM)), const((1, MLA_INNER)),
                      pl.BlockSpec(memory_space=pl.ANY), pl.BlockSpec(memory_space=pl.ANY)],
            out_specs=pl.BlockSpec((None, lq, MLA_INNER), lambda b, j, pt: (b, 0, 0)),
            scratch_shapes=[pltpu.VMEM((rows, 1), F32), pltpu.VMEM((rows, 1), F32), pltpu.VMEM((rows, KV_LORA), F32),
                            pltpu.VMEM((2, keys, KV_LORA), F32), pltpu.VMEM((2, keys, QK_ROPE), F32),
                            pltpu.SemaphoreType.DMA((2,))]),
        out_shape=jax.ShapeDtypeStruct((nb, lq, MLA_INNER), F32),
        compiler_params=_params(("arbitrary", "arbitrary")),
        name="attn_sample",
    )(page_table, qlat, qpe, c_new, k_new, w_uv_r, g_attn, pool_ckv, pool_kpe)


def _merge_router_kernel(ha_ref, ya_ref, oa_ref, hb_ref, yb_ref, ob_ref, wa_ref, wb_ref, gffn_ref, wr_ref, br_ref,
                         h2_ref, xn_ref, topi_ref, gate_ref, rank_ref, cnt_ref, carry_ref, *, tm, steps_a):
    i = pl.program_id(0)

    @pl.when(i == 0)
    def _():
        carry_ref[...] = jnp.zeros(carry_ref.shape, F32)

    first = i < steps_a
    pick = lambda a_ref, b_ref, dt: jnp.where(first, a_ref[...].astype(dt), b_ref[...].astype(dt))
    mix = (jnp.dot(pick(ya_ref, yb_ref, BF16), wa_ref[...], preferred_element_type=F32)
           + jnp.dot(pick(oa_ref, ob_ref, BF16), wb_ref[...], preferred_element_type=F32))
    h2 = pick(ha_ref, hb_ref, F32) + mix
    h2_ref[...] = h2
    xn32 = _rms(h2, gffn_ref[...])
    xn4 = _tile_view(xn_ref)
    for ct in range(ROW_TILES):
        xn4[:, ct] = xn32[:, ct * LANES:(ct + 1) * LANES].reshape(tm // SUBLANES, SUBLANES, LANES)
    xn = xn32.astype(BF16)
    lane = lax.broadcasted_iota(jnp.int32, (tm, LANES), 1)
    logits = jnp.dot(xn, wr_ref[...], preferred_element_type=F32) + br_ref[...]
    logits = jnp.where(lane < N_EXPERTS, logits, -jnp.inf)
    vals, idxs = [], []
    for _ in range(TOP_K):
        v = jnp.max(logits, axis=-1, keepdims=True)
        idx = jnp.min(jnp.where(logits == v, lane, LANES), axis=-1, keepdims=True)
        vals.append(v)
        idxs.append(idx)
        logits = jnp.where(lane == idx, -jnp.inf, logits)
    exps = [jnp.exp(v - vals[0]) for v in vals]
    denom = exps[0]
    for e in exps[1:]:
        denom = denom + e
    hit = [lane == idx for idx in idxs]
    onehot = jnp.zeros((tm, LANES), F32)
    for hk in hit:
        onehot = onehot + jnp.where(hk, 1.0, 0.0)
    r_i = lax.broadcasted_iota(jnp.int32, (tm, tm), 0)
    c_j = lax.broadcasted_iota(jnp.int32, (tm, tm), 1)
    strict_tril = jnp.where(r_i > c_j, 1.0, 0.0).astype(BF16)
    before = jnp.dot(strict_tril, onehot.astype(BF16), preferred_element_type=F32) + carry_ref[...]
    carry_ref[...] = carry_ref[...] + jnp.sum(onehot, axis=0, keepdims=True)
    topi = jnp.zeros((tm, LANES), jnp.int32)
    gate = jnp.zeros((tm, LANES), F32)
    rank = jnp.zeros((tm, LANES), jnp.int32)
    for k in range(TOP_K):
        rk = jnp.sum(jnp.where(hit[k], before, 0.0), axis=-1, keepdims=True).astype(jnp.int32)
        topi = jnp.where(lane == k, idxs[k], topi)
        gate = jnp.where(lane == k, exps[k] / denom, gate)
        rank = jnp.where(lane == k, rk, rank)
    topi_ref[...] = topi
    gate_ref[...] = gate
    rank_ref[...] = rank
    cnt_ref[...] = carry_ref[...]


def _merge_router(group_a, group_b, wts, tm):
    rows_a, rows_b = group_a[0].shape[0], group_b[0].shape[0]
    assert rows_a % tm == 0 and rows_b % tm == 0
    steps_a, steps_b = rows_a // tm, rows_b // tm
    rows = rows_a + rows_b
    spec_a = lambda w: pl.BlockSpec((tm, w), lambda i: (jnp.minimum(i, steps_a - 1), 0))
    spec_b = lambda w: pl.BlockSpec((tm, w), lambda i: (jnp.maximum(i - steps_a, 0), 0))
    row = lambda w: pl.BlockSpec((tm, w), lambda i: (i, 0))
    widths = (D_MODEL, SSD_INNER, MLA_INNER)
    sds = jax.ShapeDtypeStruct
    return pl.pallas_call(
        functools.partial(_merge_router_kernel, tm=tm, steps_a=steps_a),
        grid=(steps_a + steps_b,),
        in_specs=[spec_a(w) for w in widths] + [spec_b(w) for w in widths]
                 + [_const_spec((SSD_INNER, D_MODEL)), _const_spec((MLA_INNER, D_MODEL)), _const_spec((1, D_MODEL)),
                    _const_spec((D_MODEL, LANES)), _const_spec((1, LANES))],
        out_specs=[row(D_MODEL), _gatherable_rows_spec(tm, lambda i: i), row(LANES), row(LANES), row(LANES),
                   pl.BlockSpec((1, LANES), lambda i: (0, 0))],
        out_shape=[sds((rows, D_MODEL), F32), _gatherable_rows_shape(rows), sds((rows, LANES), jnp.int32),
                   sds((rows, LANES), F32), sds((rows, LANES), jnp.int32), sds((1, LANES), F32)],
        scratch_shapes=[pltpu.VMEM((1, LANES), F32)],
        compiler_params=_params(("arbitrary",)),
        name="merge_router",
    )(*group_a, *group_b, wts["w_out_a"], wts["w_out_b"], wts["g_ffn"], wts["w_router"], wts["b_router"])


MOE_TM = 1024
MOE_TS = 256
MOE_TF = 256
MOE_GATHER_ROWS = MOE_TM // (D_FF // MOE_TF)
COMBINE_TM = 256


def _moe_kernel(te_ref, nv_ref, src_ref, xn_hbm, wg_ref, wu_ref, wd_ref, bg_ref, bu_ref, bd_ref, y_ref,
                xland, xbf, wg_s, wu_s, wd_s, sem):
    del te_ref
    i = pl.program_id(0)
    f = pl.program_id(1)
    n_tiles = pl.num_programs(0)
    last_f = pl.num_programs(1) - 1
    nv = nv_ref[i]
    n_sub_max = MOE_TM // MOE_TS
    sub_tiles = MOE_TS // SUBLANES
    y4 = _tile_view(y_ref)
    land4 = _tile_view(xland)

    def row_copy(src_row, dst_row):
        return _gather_row_copy(xn_hbm, src_row, xland, dst_row, sem)

    def wait_tile_rows():
        for r in range(MOE_TM):
            row_copy(0, r).wait()

    @pl.when((i == 0) & (f == 0))
    def _():
        lax.fori_loop(0, MOE_TM, lambda r, c: (row_copy(src_ref[r], r).start(), c)[1], 0)

    requested = (i == 0) | (nv_ref[jnp.maximum(i - 1, 0)] > 0)

    @pl.when((f == 0) & requested)
    def _():
        wait_tile_rows()
        for s in range(n_sub_max):
            @pl.when(s * MOE_TS < nv)
            def _():
                for ct in range(ROW_TILES):
                    piece = land4[s * sub_tiles:(s + 1) * sub_tiles, ct].reshape(MOE_TS, LANES)
                    xbf[s * MOE_TS:(s + 1) * MOE_TS, ct * LANES:(ct + 1) * LANES] = piece.astype(BF16)

    for s in range(n_sub_max):
        tiles = slice(s * sub_tiles, (s + 1) * sub_tiles)

        @pl.when((f == 0) & (s * MOE_TS < nv))
        def _():
            for ct in range(ROW_TILES):
                y4[tiles, ct] = jnp.broadcast_to(bd_ref[:, ct * LANES:(ct + 1) * LANES], (sub_tiles, SUBLANES, LANES))

        @pl.when((f == 0) & (s * MOE_TS >= nv))
        def _():
            y4[tiles] = jnp.zeros((sub_tiles, ROW_TILES, SUBLANES, LANES), F32)

    def tile_step(n_sub):
        wg_s[...] = wg_ref[...].astype(BF16)
        wu_s[...] = wu_ref[...].astype(BF16)
        wd_s[...] = wd_ref[...].astype(BF16)
        for r in range(MOE_GATHER_ROWS):
            row = f * MOE_GATHER_ROWS + r
            row_copy(src_ref[(i + 1) * MOE_TM + row], row).start(priority=1)
        for s in range(n_sub):
            x = xbf[s * MOE_TS:(s + 1) * MOE_TS, :]
            g = jnp.minimum(jnp.dot(x, wg_s[...], preferred_element_type=F32) + bg_ref[...], SWIGLU_LIMIT)
            u = jnp.clip(jnp.dot(x, wu_s[...], preferred_element_type=F32) + bu_ref[...],
                         -SWIGLU_LIMIT, SWIGLU_LIMIT)
            hdn = ((u + 1.0) * (g * jax.nn.sigmoid(SWIGLU_ALPHA * g))).astype(BF16)
            part = jnp.dot(hdn, wd_s[...], preferred_element_type=F32)
            tiles = slice(s * sub_tiles, (s + 1) * sub_tiles)
            for ct in range(ROW_TILES):
                y4[tiles, ct] = y4[tiles, ct] + part[:, ct * LANES:(ct + 1) * LANES].reshape(
                    sub_tiles, SUBLANES, LANES)

    n_sub_valid = (nv + MOE_TS - 1) // MOE_TS
    for k in range(1, n_sub_max + 1):
        pl.when(n_sub_valid == k)(functools.partial(tile_step, k))

    @pl.when((i == n_tiles - 1) & (f == last_f) & (nv > 0))
    def _():
        wait_tile_rows()


def _moe_experts(tile_expert, tile_nvalid, src_rows, xn_rows, w_gate, b_gate, w_up, b_up, w_down, b_down):
    n_tiles = src_rows.shape[0] // MOE_TM - 1
    n_f = D_FF // MOE_TF
    assert MOE_GATHER_ROWS * n_f == MOE_TM
    fidx = lambda i, f, nv: jnp.where(nv[i] > 0, f, n_f - 1)
    wspec = lambda shape, imap: pl.BlockSpec(shape, lambda i, f, te, nv, src: imap(te[i], fidx(i, f, nv)))
    return pl.pallas_call(
        _moe_kernel,
        grid_spec=pltpu.PrefetchScalarGridSpec(
            num_scalar_prefetch=3, grid=(n_tiles, n_f),
            in_specs=[pl.BlockSpec(memory_space=pl.ANY),
                      wspec((None, D_MODEL, MOE_TF), lambda e, f: (e, 0, f)),
                      wspec((None, D_MODEL, MOE_TF), lambda e, f: (e, 0, f)),
                      wspec((None, MOE_TF, D_MODEL), lambda e, f: (e, f, 0)),
                      wspec((None, 1, MOE_TF), lambda e, f: (e, 0, f)),
                      wspec((None, 1, MOE_TF), lambda e, f: (e, 0, f)),
                      wspec((None, 1, D_MODEL), lambda e, f: (e, 0, 0))],
            out_specs=_gatherable_rows_spec(MOE_TM, lambda i, f, te, nv, src: i),
            scratch_shapes=[_row_landing_scratch(MOE_TM), pltpu.VMEM((MOE_TM, D_MODEL), BF16),
                            pltpu.VMEM((D_MODEL, MOE_TF), BF16), pltpu.VMEM((D_MODEL, MOE_TF), BF16),
                            pltpu.VMEM((MOE_TF, D_MODEL), BF16), pltpu.SemaphoreType.DMA(())]),
        out_shape=_gatherable_rows_shape(n_tiles * MOE_TM),
        compiler_params=_params(("arbitrary", "arbitrary")),
        name="moe_experts",
    )(tile_expert, tile_nvalid, src_rows, xn_rows, w_gate, w_up, w_down,
      b_gate.reshape(N_EXPERTS, 1, D_FF), b_up.reshape(N_EXPERTS, 1, D_FF), b_down.reshape(N_EXPERTS, 1, D_MODEL))


def _combine_kernel(pos_ref, h2_ref, gate_ref, y_hbm, gfin_ref, out_ref, *scratch, tm, row_offset):
    bufs, sems = scratch[:2 * TOP_K], scratch[2 * TOP_K]
    i = pl.program_id(0)
    n = pl.num_programs(0)

    def row_copy(src_row, slot, k, t):
        return _gather_row_copy(y_hbm, src_row, bufs[slot * TOP_K + k], t, sems.at[slot])

    def start_copies(step, slot):
        def body(t, carry):
            for k in range(TOP_K):
                row_copy(pos_ref[(row_offset + step * tm + t) * TOP_K + k], slot, k, t).start(priority=k % 2)
            return carry
        lax.fori_loop(0, tm, body, 0, unroll=2)

    def wait_copies(slot):
        for t in range(tm):
            for k in range(TOP_K):
                row_copy(0, slot, k, t).wait()

    def for_slot(slot_value, fn):
        for slot in range(2):
            pl.when(slot_value == slot)(functools.partial(fn, slot))

    @pl.when(i == 0)
    def _():
        start_copies(0, 0)

    @pl.when(i + 1 < n)
    def _():
        for_slot((i + 1) % 2, lambda slot: start_copies(i + 1, slot))

    def finish(slot):
        wait_copies(slot)
        gate = gate_ref[...]
        gates = [jnp.broadcast_to(gate[:, k:k + 1], (tm, LANES)) for k in range(TOP_K)]
        views = [_tile_view(bufs[slot * TOP_K + k]) for k in range(TOP_K)]
        parts = []
        for ct in range(ROW_TILES):
            acc = h2_ref[:, ct * LANES:(ct + 1) * LANES]
            for k in range(TOP_K):
                acc = acc + gates[k] * views[k][:, ct].reshape(tm, LANES)
            parts.append(acc)
        out_ref[...] = _rms(jnp.concatenate(parts, axis=-1), gfin_ref[...])

    for_slot(i % 2, finish)


def _combine(pos_flat, h2, gate, y_rows, g_final, row_offset, rows):
    tm = COMBINE_TM
    assert rows % tm == 0 and row_offset % tm == 0
    off = row_offset // tm
    return pl.pallas_call(
        functools.partial(_combine_kernel, tm=tm, row_offset=row_offset),
        grid_spec=pltpu.PrefetchScalarGridSpec(
            num_scalar_prefetch=1, grid=(rows // tm,),
            in_specs=[pl.BlockSpec((tm, D_MODEL), lambda i, pos: (i + off, 0)),
                      pl.BlockSpec((tm, LANES), lambda i, pos: (i + off, 0)),
                      pl.BlockSpec(memory_space=pl.ANY),
                      pl.BlockSpec((1, D_MODEL), lambda i, pos: (0, 0))],
            out_specs=pl.BlockSpec((tm, D_MODEL), lambda i, pos: (i, 0)),
            scratch_shapes=[_row_landing_scratch(tm) for _ in range(2 * TOP_K)] + [pltpu.SemaphoreType.DMA((2,))]),
        out_shape=jax.ShapeDtypeStruct((rows, D_MODEL), F32),
        compiler_params=_params(("arbitrary",)),
        name="moe_combine",
    )(pos_flat, h2, gate, y_rows, g_final)


def _routing_tables(topi, rank, counts, n_rows):
    counts = counts[0, :N_EXPERTS].astype(jnp.int32)
    tiles_per = (counts + MOE_TM - 1) // MOE_TM
    tile_end = jnp.cumsum(tiles_per)
    tile_start = tile_end - tiles_per
    n_tiles_max = (n_rows * TOP_K) // MOE_TM + N_EXPERTS
    pos = tile_start[topi] * MOE_TM + rank
    src = jnp.zeros(((n_tiles_max + 1) * MOE_TM,), jnp.int32).at[pos.reshape(-1)].set(
        jnp.repeat(jnp.arange(n_rows, dtype=jnp.int32), TOP_K))
    tile_id = jnp.arange(n_tiles_max, dtype=jnp.int32)
    n_used = tile_end[-1]
    last = jnp.maximum(n_used - 1, 0)
    tile_row = jnp.minimum(tile_id, last)
    owner = jnp.sum((tile_end[None, :] <= tile_row[:, None]).astype(jnp.int32), axis=1)
    tile_expert = jnp.minimum(owner, N_EXPERTS - 1)
    in_tile = tile_row - tile_start[tile_expert]
    nvalid = jnp.clip(counts[tile_expert] - in_tile * MOE_TM, 0, MOE_TM)
    nvalid = jnp.where(tile_id < n_used, nvalid, 0).astype(jnp.int32)
    return pos.reshape(-1).astype(jnp.int32), src, tile_expert, nvalid


def kernel(x_prompt, x_sample, cache_ckv, cache_kpe, state_ssm, state_conv, page_table, meta_tokens, g_mix_norm, w_in, g_q_norm, w_q_up, g_kv_norm, w_uk, w_uv, conv_w, conv_b, dt_bias, a_log, d_skip, g_ssd_norm, g_attn_out, w_out, g_ffn_norm, w_router, b_router, w_gate, b_gate, w_up, b_up, w_down, b_down, g_final_norm):
    assert w_in.shape[0] == 1, "single-layer model"
    bp, seq = x_prompt.shape[:2]
    bs, dseq = x_sample.shape[:2]
    rows_p, rows_s = bp * seq, bs * dseq
    past_len = page_table.shape[1] * cache_ckv.shape[2]
    wm = _prep_mixer_weights(g_mix_norm[0], w_in[0], g_q_norm[0], w_q_up[0], g_kv_norm[0], w_uk[0])
    ws = _prep_ssd_weights(conv_w[0], conv_b[0], dt_bias[0], a_log[0], d_skip[0], g_ssd_norm[0])
    w_uv_r = jnp.transpose(w_uv[0], (1, 0, 2)).astype(BF16)
    g_attn = g_attn_out[0].reshape(1, MLA_INNER)
    wr = {"w_out_a": w_out[0][:SSD_INNER].astype(BF16), "w_out_b": w_out[0][SSD_INNER:].astype(BF16),
          "g_ffn": g_ffn_norm[0].reshape(1, D_MODEL),
          "w_router": jnp.pad(w_router[0], ((0, 0), (0, LANES - N_EXPERTS))).astype(BF16),
          "b_router": jnp.pad(b_router[0], (0, LANES - N_EXPERTS)).reshape(1, LANES)}

    tm = 256
    a_m = _inproj(meta_tokens, *_rope_tables(jnp.arange(N_META)), wm, N_META)
    a_p = _inproj(x_prompt.reshape(rows_p, D_MODEL), *_rope_tables(N_META + jnp.arange(seq)), wm, tm)
    cs_s = [jnp.tile(t, (tm // dseq, 1)) for t in _rope_tables(past_len + jnp.arange(dseq))]
    a_s = _inproj(x_sample.reshape(rows_s, D_MODEL), *cs_s, wm, tm)
    qlat_m, qpe_m, ckv_m, ckvb_m, kpe_m, kpeb_m, z_m, xbc_m, dt16_m, dtx_m = a_m
    qlat_p, qpe_p, ckv_p, ckvb_p, kpe_p, kpeb_p, z_p, xbc_p, dt16_p, dtx_p = a_p
    qlat_s, qpe_s, ckv_s, ckvb_s, kpe_s, kpeb_s, z_s, xbc_s, dt16_s, dtx_s = a_s
    del qlat_m, qpe_m, ckvb_s, kpeb_s, dt16_s

    heads_first = lambda q, b, t: q.reshape(MLA_HEADS, b, t, q.shape[-1])
    o_p = _attn_prompt(heads_first(qlat_p, bp, seq), heads_first(qpe_p, bp, seq),
                       ckvb_p.reshape(bp, seq, KV_LORA), kpeb_p.reshape(bp, seq, QK_ROPE),
                       ckvb_m, kpeb_m, w_uv_r, g_attn, 256)
    per_seq = lambda q: jnp.transpose(heads_first(q, bs, dseq), (1, 0, 2, 3)).reshape(bs, MLA_HEADS * dseq, q.shape[-1])
    o_s = _attn_sample(page_table, per_seq(qlat_s), per_seq(qpe_s), ckv_s.reshape(bs, dseq, KV_LORA),
                       kpe_s.reshape(bs, dseq, QK_ROPE), cache_ckv[0], cache_kpe[0], w_uv_r, g_attn)

    pad_meta = lambda v: jnp.pad(v, ((0, CHUNK - N_META), (0, 0)))[None]
    _, ssm_m, conv_m = _ssd_chunked(pad_meta(z_m), pad_meta(xbc_m), pad_meta(dt16_m), pad_meta(dtx_m),
                                    jnp.zeros((1, CONV_W - 1, CONV_DIM), F32),
                                    jnp.zeros((1, SSD_INNER, D_STATE), F32), ws, CHUNK, N_META, True)
    per_b = lambda v, b, t: v.reshape(b, t, v.shape[-1])
    y_p, ssm_p, conv_p = _ssd_chunked(per_b(z_p, bp, seq), per_b(xbc_p, bp, seq), per_b(dt16_p, bp, seq),
                                      per_b(dtx_p, bp, seq), conv_m, ssm_m, ws, CHUNK, CHUNK, True)
    y_s, ssm_s, conv_s = _ssd_step(per_b(z_s, bs, dseq), per_b(xbc_s, bs, dseq), per_b(dtx_s, bs, dseq),
                                   state_conv[0], state_ssm[0].reshape(bs, SSD_INNER, D_STATE), ws, 8)

    total = rows_p + rows_s
    h2, xn_rows, topi, gate, rank, counts = _merge_router(
        (x_prompt.reshape(rows_p, D_MODEL), y_p.reshape(rows_p, SSD_INNER), o_p.reshape(rows_p, MLA_INNER)),
        (x_sample.reshape(rows_s, D_MODEL), y_s.reshape(rows_s, SSD_INNER), o_s.reshape(rows_s, MLA_INNER)),
        wr, tm)

    pos, src, tile_expert, tile_nvalid = _routing_tables(topi[:, :TOP_K], rank[:, :TOP_K], counts, total)
    y_rows = _moe_experts(tile_expert, tile_nvalid, src, xn_rows, w_gate[0], b_gate[0], w_up[0],
                          b_up[0], w_down[0], b_down[0])
    g_fin = g_final_norm.reshape(1, D_MODEL)
    y_prompt = _combine(pos, h2, gate, y_rows, g_fin, 0, rows_p).reshape(bp, seq, D_MODEL)
    y_sample = _combine(pos, h2, gate, y_rows, g_fin, rows_p, rows_s).reshape(bs, dseq, D_MODEL)

    with_meta = lambda m, p, w: jnp.concatenate(
        [jnp.broadcast_to(m[None], (bp, N_META, w)), p.reshape(bp, seq, w)], axis=1)[None]
    return (y_prompt, y_sample,
            with_meta(ckv_m, ckv_p, KV_LORA), with_meta(kpe_m, kpe_p, QK_ROPE),
            ssm_p.reshape(1, bp, SSD_HEADS, SSD_HEAD_DIM, D_STATE), conv_p[None],
            ckv_s.reshape(1, bs, dseq, KV_LORA), kpe_s.reshape(1, bs, dseq, QK_ROPE),
            ssm_s.reshape(1, bs, SSD_HEADS, SSD_HEAD_DIM, D_STATE), conv_s[None])
```

```python
import functools
import math

import jax
import jax.numpy as jnp
from jax import lax
from jax.experimental import pallas as pl
from jax.experimental.pallas import tpu as pltpu

F32 = jnp.float32
BF16 = jnp.bfloat16

D_MODEL = 2048
N_META = 16
EPS = 1e-6
SSD_HEADS = 16
SSD_HEAD_DIM = 64
SSD_INNER = SSD_HEADS * SSD_HEAD_DIM
SSD_GROUPS = 2
SSD_HPG = SSD_HEADS // SSD_GROUPS
D_STATE = 128
CONV_W = 4
CONV_DIM = SSD_INNER + 2 * SSD_GROUPS * D_STATE
CHUNK = 128
MLA_HEADS = 8
Q_LORA = 512
KV_LORA = 512
QK_NOPE = 128
QK_ROPE = 64
V_DIM = 128
MLA_INNER = MLA_HEADS * V_DIM
ROPE_THETA = 10000.0
SM_SCALE = (QK_NOPE + QK_ROPE) ** -0.5
N_EXPERTS = 32
TOP_K = 4
D_FF = 2048
SWIGLU_LIMIT = 7.0
SWIGLU_ALPHA = 1.702

LANES = 128
SUBLANES = 8
ROW_TILES = D_MODEL // LANES
VMEM_LIMIT_BYTES = 56 * 1024 * 1024

U_QA = 0
U_KV = U_QA + Q_LORA
U_Z = U_KV + KV_LORA
U_XBC = U_Z + SSD_INNER
U_KPE = U_XBC + CONV_DIM
U_TAIL = U_KPE + 2 * QK_ROPE
U_DTX = U_TAIL + LANES
U_WIDTH = U_DTX + SSD_INNER
Q_NOPE_W = MLA_HEADS * QK_NOPE
Q_ROPE_W = MLA_HEADS * QK_ROPE
Q_UP_WIDTH = Q_NOPE_W + 2 * Q_ROPE_W


def _params(semantics):
    return pltpu.CompilerParams(dimension_semantics=semantics, vmem_limit_bytes=VMEM_LIMIT_BYTES)


def _const_spec(shape):
    nd = len(shape)
    return pl.BlockSpec(shape, lambda *_: (0,) * nd, pipeline_mode=pl.Buffered(1))


def _rms(x, g):
    return x * lax.rsqrt(jnp.mean(x * x, axis=-1, keepdims=True) + EPS) * g


def _gatherable_rows_shape(rows):
    return jax.ShapeDtypeStruct((rows // SUBLANES, ROW_TILES, SUBLANES, 1, LANES), F32)


def _gatherable_rows_spec(tm, row_block):
    return pl.BlockSpec((tm // SUBLANES, ROW_TILES, SUBLANES, 1, LANES), lambda *a: (row_block(*a), 0, 0, 0, 0))


def _tile_view(ref):
    return ref.reshape(ref.shape[0], ROW_TILES, SUBLANES, LANES)


def _row_landing_scratch(rows):
    return pltpu.VMEM((rows // SUBLANES, ROW_TILES, SUBLANES, 1, LANES), F32)


def _gather_row_copy(src, src_row, dst, dst_row, sem):
    def split(r):
        if isinstance(r, int):
            return r // SUBLANES, r % SUBLANES
        return lax.shift_right_logical(r, 3), lax.bitwise_and(r, SUBLANES - 1)

    s_rt, s_sl = split(src_row)
    d_rt, d_sl = split(dst_row)
    return pltpu.make_async_copy(src.at[s_rt, :, s_sl], dst.at[d_rt, :, d_sl], sem)


def _rot_half_cols(w):
    half = QK_ROPE // 2
    return jnp.concatenate([-w[..., half:], w[..., :half]], axis=-1)


def _inproj_kernel(x_ref, gmix_ref, win_ref, gq_ref, wq_ref, wuk_ref, gkv_ref, cs_q_ref, cs_k_ref,
                   qlat_ref, qpe_ref, ckv_ref, ckvb_ref, kpe_ref, kpeb_ref, z_ref, xbc_ref, tail_ref, dtx_ref):
    xn = _rms(x_ref[...], gmix_ref[...]).astype(BF16)
    u = jnp.dot(xn, win_ref[...], preferred_element_type=F32)
    z_ref[...] = u[:, U_Z:U_Z + SSD_INNER]
    xbc_ref[...] = u[:, U_XBC:U_XBC + CONV_DIM]
    tail_ref[...] = u[:, U_TAIL:U_TAIL + LANES]
    dtx_ref[...] = u[:, U_DTX:U_DTX + SSD_INNER]
    ckv = _rms(u[:, U_KV:U_KV + KV_LORA], gkv_ref[...])
    ckv_ref[...] = ckv
    ckvb_ref[...] = ckv.astype(BF16)
    kprod = u[:, U_KPE:U_KPE + 2 * QK_ROPE] * cs_k_ref[...]
    kpe = kprod[:, :QK_ROPE] + kprod[:, QK_ROPE:]
    kpe_ref[...] = kpe
    kpeb_ref[...] = kpe.astype(BF16)
    qn = _rms(u[:, U_QA:U_QA + Q_LORA], gq_ref[...]).astype(BF16)
    q = jnp.dot(qn, wq_ref[...], preferred_element_type=F32)
    cs_q = cs_q_ref[...]
    qpe = (q[:, Q_NOPE_W:Q_NOPE_W + Q_ROPE_W] * cs_q[:, :Q_ROPE_W]
           + q[:, Q_NOPE_W + Q_ROPE_W:] * cs_q[:, Q_ROPE_W:]).astype(BF16)
    for h in range(MLA_HEADS):
        qpe_ref[h] = qpe[:, h * QK_ROPE:(h + 1) * QK_ROPE]
        qn_h = q[:, h * QK_NOPE:(h + 1) * QK_NOPE].astype(BF16)
        qlat_ref[h] = jnp.dot(qn_h, wuk_ref[h], preferred_element_type=F32).astype(BF16)


def _inproj(x, cs_q, cs_k, wts, tm):
    rows = x.shape[0]
    pos_rows = cs_q.shape[0]
    assert rows % tm == 0 and pos_rows % tm == 0
    n_pos = pos_rows // tm
    row = lambda w: pl.BlockSpec((tm, w), lambda i: (i, 0))
    pos = lambda w: pl.BlockSpec((tm, w), lambda i: (i % n_pos, 0))
    head = lambda w: pl.BlockSpec((MLA_HEADS, tm, w), lambda i: (0, i, 0))
    sds = jax.ShapeDtypeStruct
    return pl.pallas_call(
        _inproj_kernel,
        grid=(rows // tm,),
        in_specs=[row(D_MODEL), _const_spec((1, D_MODEL)), _const_spec((D_MODEL, U_WIDTH)),
                  _const_spec((1, Q_LORA)), _const_spec((Q_LORA, Q_UP_WIDTH)),
                  _const_spec((MLA_HEADS, QK_NOPE, KV_LORA)), _const_spec((1, KV_LORA)),
                  pos(2 * Q_ROPE_W), pos(2 * QK_ROPE)],
        out_specs=[head(KV_LORA), head(QK_ROPE), row(KV_LORA), row(KV_LORA), row(QK_ROPE), row(QK_ROPE),
                   row(SSD_INNER), row(CONV_DIM), row(LANES), row(SSD_INNER)],
        out_shape=[sds((MLA_HEADS, rows, KV_LORA), BF16), sds((MLA_HEADS, rows, QK_ROPE), BF16),
                   sds((rows, KV_LORA), F32), sds((rows, KV_LORA), BF16),
                   sds((rows, QK_ROPE), F32), sds((rows, QK_ROPE), BF16),
                   sds((rows, SSD_INNER), F32), sds((rows, CONV_DIM), F32), sds((rows, LANES), F32),
                   sds((rows, SSD_INNER), F32)],
        compiler_params=_params(("arbitrary",)),
        name="inproj",
    )(x, wts["g_mix"], wts["w_in"], wts["g_q"], wts["w_q"], wts["w_uk"], wts["g_kv"], cs_q, cs_k)


def _rope_tables(pos):
    inv_freq = ROPE_THETA ** (-jnp.arange(0, QK_ROPE, 2, dtype=F32) / QK_ROPE)
    ang = pos.astype(F32)[:, None] * inv_freq[None, :]
    cos = jnp.tile(jnp.cos(ang), (1, 2))
    sin = jnp.tile(jnp.sin(ang), (1, 2))
    cs_k = jnp.concatenate([cos, sin], axis=-1)
    cs_q = jnp.concatenate([jnp.tile(cos, (1, MLA_HEADS)), jnp.tile(sin, (1, MLA_HEADS))], axis=-1)
    return cs_q, cs_k


def _prep_mixer_weights(g_mix_norm, w_in, g_q_norm, w_q_up, g_kv_norm, w_uk):
    s = (0, Q_LORA, Q_LORA + KV_LORA, Q_LORA + KV_LORA + QK_ROPE,
         Q_LORA + KV_LORA + QK_ROPE + SSD_INNER, Q_LORA + KV_LORA + QK_ROPE + SSD_INNER + CONV_DIM)
    w_qa, w_kv, w_kpe, w_z, w_xbc, w_dt = (w_in[:, s[i]:(s[i + 1] if i + 1 < len(s) else None)] for i in range(6))
    pad = jnp.zeros((D_MODEL, LANES - SSD_HEADS), w_in.dtype)
    w_in_r = jnp.concatenate([w_qa, w_kv, w_z, w_xbc, w_kpe, _rot_half_cols(w_kpe), w_dt, pad,
                              jnp.repeat(w_dt, SSD_HEAD_DIM, axis=1)], axis=1)
    wq = w_q_up.reshape(Q_LORA, MLA_HEADS, QK_NOPE + QK_ROPE)
    wq_nope = wq[:, :, :QK_NOPE].reshape(Q_LORA, Q_NOPE_W)
    wq_rope = wq[:, :, QK_NOPE:]
    w_q_r = jnp.concatenate([wq_nope, wq_rope.reshape(Q_LORA, Q_ROPE_W),
                             _rot_half_cols(wq_rope).reshape(Q_LORA, Q_ROPE_W)], axis=1)
    return {
        "g_mix": g_mix_norm.reshape(1, D_MODEL), "w_in": w_in_r.astype(BF16),
        "g_q": g_q_norm.reshape(1, Q_LORA), "w_q": w_q_r.astype(BF16),
        "w_uk": jnp.transpose(w_uk, (1, 2, 0)).astype(BF16),
        "g_kv": g_kv_norm.reshape(1, KV_LORA),
    }


def _nt_dot(a, b):
    return lax.dot_general(a, b, (((1,), (1,)), ((), ())), preferred_element_type=F32)


def _attn_out_proj(o_lat, wuv_ref, g_ref, rows):
    o_b = o_lat.astype(BF16)
    o = jnp.concatenate([jnp.dot(o_b[h * rows:(h + 1) * rows], wuv_ref[h], preferred_element_type=F32)
                         for h in range(MLA_HEADS)], axis=-1)
    return _rms(o, g_ref[...])


def _attn_prompt_kernel(qlat_ref, qpe_ref, ckv_ref, kpe_ref, ckvm_ref, kpem_ref, wuv_ref, g_ref,
                        o_ref, m_ref, l_ref, acc_ref, s_ref, *, bq):
    qi = pl.program_id(1)
    rows = MLA_HEADS * bq
    q_lat = qlat_ref[...].reshape(rows, KV_LORA)
    q_pe = qpe_ref[...].reshape(rows, QK_ROPE)

    def scores(c, kp):
        return (_nt_dot(q_lat, c) + _nt_dot(q_pe, kp)) * SM_SCALE

    cm = ckvm_ref[...]
    s = scores(cm, kpem_ref[...])
    m0 = jnp.max(s, axis=-1, keepdims=True)
    p = jnp.exp(s - m0)
    m_ref[...] = m0
    l_ref[...] = jnp.sum(p, axis=-1, keepdims=True)
    acc_ref[...] = jnp.dot(p.astype(BF16), cm, preferred_element_type=F32)

    def block_scores(j):
        k0 = pl.multiple_of(j * bq, bq)
        return scores(ckv_ref[pl.ds(k0, bq), :], kpe_ref[pl.ds(k0, bq), :])

    def absorb(s, j, masked):
        if masked:
            causal = (lax.broadcasted_iota(jnp.int32, (bq, bq), 0) >= lax.broadcasted_iota(jnp.int32, (bq, bq), 1))
            s = jnp.where(causal[None], s.reshape(MLA_HEADS, bq, bq), -jnp.inf).reshape(rows, bq)
        m_old = m_ref[...]
        m_new = jnp.maximum(m_old, jnp.max(s, axis=-1, keepdims=True))
        corr = jnp.exp(m_old - m_new)
        p = jnp.exp(s - m_new)
        m_ref[...] = m_new
        l_ref[...] = l_ref[...] * corr + jnp.sum(p, axis=-1, keepdims=True)
        c = ckv_ref[pl.ds(pl.multiple_of(j * bq, bq), bq), :]
        acc_ref[...] = acc_ref[...] * corr + jnp.dot(p.astype(BF16), c, preferred_element_type=F32)

    s_ref[...] = block_scores(0)

    def full_block(j, carry):
        s = s_ref[...]
        s_next = block_scores(j + 1)
        absorb(s, j, False)
        s_ref[...] = s_next
        return carry

    lax.fori_loop(0, qi, full_block, 0)
    absorb(s_ref[...], qi, True)
    o_ref[...] = _attn_out_proj(acc_ref[...] / l_ref[...], wuv_ref, g_ref, bq).astype(o_ref.dtype)


def _attn_prompt(qlat, qpe, ckvb, kpeb, ckvb_meta, kpeb_meta, w_uv_r, g_attn, bq):
    _, nb, seq, _ = qlat.shape
    assert seq % bq == 0
    rows = MLA_HEADS * bq
    return pl.pallas_call(
        functools.partial(_attn_prompt_kernel, bq=bq),
        grid=(nb, seq // bq),
        in_specs=[pl.BlockSpec((MLA_HEADS, None, bq, KV_LORA), lambda b, i: (0, b, i, 0)),
                  pl.BlockSpec((MLA_HEADS, None, bq, QK_ROPE), lambda b, i: (0, b, i, 0)),
                  pl.BlockSpec((None, seq, KV_LORA), lambda b, i: (b, 0, 0)),
                  pl.BlockSpec((None, seq, QK_ROPE), lambda b, i: (b, 0, 0)),
                  _const_spec((N_META, KV_LORA)), _const_spec((N_META, QK_ROPE)),
                  _const_spec((MLA_HEADS, KV_LORA, V_DIM)), _const_spec((1, MLA_INNER))],
        out_specs=pl.BlockSpec((None, bq, MLA_INNER), lambda b, i: (b, i, 0)),
        out_shape=jax.ShapeDtypeStruct((nb, seq, MLA_INNER), BF16),
        scratch_shapes=[pltpu.VMEM((rows, 1), F32), pltpu.VMEM((rows, 1), F32), pltpu.VMEM((rows, KV_LORA), F32),
                        pltpu.VMEM((rows, bq), F32)],
        compiler_params=_params(("arbitrary", "arbitrary")),
        name="attn_prompt",
    )(qlat, qpe, ckvb, kpeb, ckvb_meta, kpeb_meta, w_uv_r, g_attn)


HIST_ROWS = 8
GROUP_W = SSD_INNER // SSD_GROUPS


def _silu(x):
    return x * jax.nn.sigmoid(x)


def _softplus(x):
    return jnp.maximum(x, 0.0) + jnp.log1p(jnp.exp(-jnp.abs(x)))


def _split3_dot(m_bf16, a):
    hi = a.astype(BF16)
    r1 = a - hi.astype(F32)
    mid = r1.astype(BF16)
    lo = (r1 - mid.astype(F32)).astype(BF16)
    dot = lambda t: jnp.dot(m_bf16, t, preferred_element_type=F32)
    return dot(hi) + dot(mid) + dot(lo)


def _conv_silu(xpad_ref, xbc, hist_rows_ref, cw_ref, cb_ref, first, lc, valid):
    h0 = HIST_ROWS - (CONV_W - 1)

    @pl.when(first)
    def _():
        xpad_ref[h0:HIST_ROWS, :] = hist_rows_ref[...]

    xpad_ref[HIST_ROWS:HIST_ROWS + lc, :] = xbc
    acc = cb_ref[...] + cw_ref[CONV_W - 1:CONV_W, :] * xbc
    for k in range(CONV_W - 1):
        acc = acc + cw_ref[k:k + 1, :] * xpad_ref[h0 + k:h0 + k + lc, :]
    tail = xpad_ref[h0 + valid:HIST_ROWS + valid, :]
    xpad_ref[h0:HIST_ROWS, :] = tail
    return _silu(acc), tail


def _gated_group_norm(y, z, g):
    yg = y * _silu(z)
    parts = []
    for gi in range(SSD_GROUPS):
        part = yg[:, gi * GROUP_W:(gi + 1) * GROUP_W]
        parts.append(part * lax.rsqrt(jnp.mean(part * part, axis=-1, keepdims=True) + EPS))
    return jnp.concatenate(parts, axis=-1) * g


def _ssd_chunk_kernel(z_ref, xbc_ref, dt16_ref, dtx_ref, hist_ref, s0_ref, cw_ref, cb_ref,
                      dtb16_ref, alog16_ref, dtbx_ref, alogx_ref, dskx_ref, g_ref,
                      y_ref, sout_ref, cout_ref, st_ref, xpad_ref, *, lc, valid):
    c = pl.program_id(1)

    @pl.when(c == 0)
    def _():
        st_ref[...] = s0_ref[...].T

    xc, tail = _conv_silu(xpad_ref, xbc_ref[...], hist_ref, cw_ref, cb_ref, c == 0, lc, valid)
    xs = xc[:, :SSD_INNER]
    row_i = lax.broadcasted_iota(jnp.int32, (lc, lc), 0)
    col_j = lax.broadcasted_iota(jnp.int32, (lc, lc), 1)
    causal = row_i >= col_j
    tril = jnp.where(causal, 1.0, 0.0).astype(BF16)

    def dt_and_cumsum(raw_ref, bias_ref, alog_ref):
        dt = _softplus(raw_ref[...] + bias_ref[...])
        if valid < lc:
            dt = jnp.where(lax.broadcasted_iota(jnp.int32, dt.shape, 0) < valid, dt, 0.0)
        return dt, _split3_dot(tril, dt * (-jnp.exp(alog_ref[...])))

    _, acs16 = dt_and_cumsum(dt16_ref, dtb16_ref, alog16_ref)
    dtx, acsx = dt_and_cumsum(dtx_ref, dtbx_ref, alogx_ref)
    acs16_t = acs16.T
    xdt = (xs * dtx).astype(BF16)
    lane = lax.broadcasted_iota(jnp.int32, (lc, 2 * SSD_HEAD_DIM), 1)
    low_half = lane < SSD_HEAD_DIM
    zero = jnp.zeros((lc, 2 * SSD_HEAD_DIM), BF16)
    st = st_ref[...]
    decay_out = jnp.exp(acsx[lc - 1:lc, :] - acsx)
    xw = (xs * decay_out * dtx).astype(BF16)
    y_parts, st_parts = [], []
    for gi in range(SSD_GROUPS):
        b0 = SSD_INNER + gi * D_STATE
        c0 = SSD_INNER + SSD_GROUPS * D_STATE + gi * D_STATE
        bm = xc[:, b0:b0 + D_STATE].astype(BF16)
        cm = xc[:, c0:c0 + D_STATE].astype(BF16)
        cb = _nt_dot(cm, bm)
        lanes = slice(gi * GROUP_W, (gi + 1) * GROUP_W)
        y_off = jnp.dot(cm, st[:, lanes].astype(BF16), preferred_element_type=F32) * jnp.exp(acsx[:, lanes])
        for pair in range(SSD_HPG // 2):
            h0 = gi * SSD_HPG + 2 * pair
            pl_lanes = slice(h0 * SSD_HEAD_DIM, (h0 + 2) * SSD_HEAD_DIM)
            xpair = xdt[:, pl_lanes]
            y_pair = None
            for t in range(2):
                h = h0 + t
                seg = acs16[:, h:h + 1] - acs16_t[h:h + 1, :]
                m = (cb * jnp.exp(jnp.where(causal, seg, -jnp.inf))).astype(BF16)
                x_h = jnp.where(low_half, xpair, zero) if t == 0 else jnp.where(low_half, zero, xpair)
                y_h = jnp.dot(m, x_h, preferred_element_type=F32)
                y_pair = y_h if y_pair is None else y_pair + y_h
            y_parts.append(y_pair + y_off[:, pl_lanes.start - lanes.start:pl_lanes.stop - lanes.start])
        inc = lax.dot_general(bm, xw[:, lanes], (((0,), (0,)), ((), ())), preferred_element_type=F32)
        st_parts.append(st[:, lanes] * jnp.exp(acsx[lc - 1:lc, lanes]) + inc)
    st_new = jnp.concatenate(st_parts, axis=-1)
    st_ref[...] = st_new
    y = jnp.concatenate(y_parts, axis=-1) + dskx_ref[...] * xs
    y_ref[...] = _gated_group_norm(y, z_ref[...], g_ref[...]).astype(y_ref.dtype)

    @pl.when(c == pl.num_programs(1) - 1)
    def _():
        sout_ref[...] = st_new.T
        cout_ref[...] = tail


def _ssd_chunked(z, xbc, dt16, dtx, hist, s0, wts, lc, valid, shared_init):
    nb, t, _ = z.shape
    assert t % lc == 0
    tok = lambda w: pl.BlockSpec((None, lc, w), lambda b, c: (b, c, 0))
    init_idx = (lambda b, c: (0, 0, 0)) if shared_init else (lambda b, c: (b, 0, 0))
    per_b = lambda r, w: pl.BlockSpec((None, r, w), lambda b, c: (b, 0, 0))
    vec = lambda w: _const_spec((1, w))
    sds = jax.ShapeDtypeStruct
    return pl.pallas_call(
        functools.partial(_ssd_chunk_kernel, lc=lc, valid=valid),
        grid=(nb, t // lc),
        in_specs=[tok(SSD_INNER), tok(CONV_DIM), tok(LANES), tok(SSD_INNER),
                  pl.BlockSpec((None, CONV_W - 1, CONV_DIM), init_idx),
                  pl.BlockSpec((None, SSD_INNER, D_STATE), init_idx),
                  _const_spec((CONV_W, CONV_DIM)), vec(CONV_DIM), vec(LANES), vec(LANES),
                  vec(SSD_INNER), vec(SSD_INNER), vec(SSD_INNER), vec(SSD_INNER)],
        out_specs=[tok(SSD_INNER), per_b(SSD_INNER, D_STATE), per_b(CONV_W - 1, CONV_DIM)],
        out_shape=[sds((nb, t, SSD_INNER), BF16), sds((nb, SSD_INNER, D_STATE), F32),
                   sds((nb, CONV_W - 1, CONV_DIM), F32)],
        scratch_shapes=[pltpu.VMEM((D_STATE, SSD_INNER), F32), pltpu.VMEM((HIST_ROWS + lc, CONV_DIM), F32)],
        compiler_params=_params(("arbitrary", "arbitrary")),
        name="ssd_chunked",
    )(z, xbc, dt16, dtx, hist, s0, wts["conv_w"], wts["conv_b"], wts["dtb16"], wts["alog16"],
      wts["dtbx"], wts["alogx"], wts["dskx"], wts["g_ssd"])


def _ssd_step_kernel(z_ref, xbc_ref, dtx_ref, hist_ref, s0_ref, cw_ref, cb_ref, dtbx_ref, alogx_ref, dskx_ref, g_ref,
                     y_ref, sout_ref, cout_ref, *, nseq, lc):
    hist_n = CONV_W - 1
    t_idx = lax.broadcasted_iota(jnp.int32, (lc, SSD_INNER), 0)
    t_idx_g = lax.broadcasted_iota(jnp.int32, (lc, GROUP_W), 0)
    a_neg = -jnp.exp(alogx_ref[...])
    for b in range(nseq):
        xbc = xbc_ref[b]
        xp = jnp.concatenate([hist_ref[b], xbc], axis=0)
        acc = cb_ref[...] + cw_ref[hist_n:CONV_W, :] * xbc
        for k in range(hist_n):
            acc = acc + cw_ref[k:k + 1, :] * xp[k:k + lc]
        xc = _silu(acc)
        cout_ref[b] = xp[lc:lc + hist_n]
        xs = xc[:, :SSD_INNER]
        dt = _softplus(dtx_ref[b] + dtbx_ref[...])
        a = dt * a_neg
        acs = jnp.zeros_like(a)
        for l in range(lc):
            acs = acs + jnp.where(t_idx >= l, a[l:l + 1, :], 0.0)
        xdt = xs * dt
        st = s0_ref[b].T
        decay_last = jnp.exp(acs[lc - 1:lc, :])
        xw = (xs * jnp.exp(acs[lc - 1:lc, :] - acs) * dt).astype(BF16)
        y_parts, st_parts = [], []
        for gi in range(SSD_GROUPS):
            b0 = SSD_INNER + gi * D_STATE
            c0 = SSD_INNER + SSD_GROUPS * D_STATE + gi * D_STATE
            bm = xc[:, b0:b0 + D_STATE]
            cm = xc[:, c0:c0 + D_STATE]
            lanes = slice(gi * GROUP_W, (gi + 1) * GROUP_W)
            acs_g = acs[:, lanes]
            y_g = jnp.dot(cm.astype(BF16), st[:, lanes].astype(BF16), preferred_element_type=F32) * jnp.exp(acs_g)
            for j in range(lc):
                cb_j = jnp.sum(cm * bm[j:j + 1, :], axis=-1, keepdims=True)
                decay = jnp.exp(jnp.where(t_idx_g >= j, acs_g - acs_g[j:j + 1, :], -jnp.inf))
                y_g = y_g + cb_j * decay * xdt[j:j + 1, lanes]
            y_parts.append(y_g)
            inc = lax.dot_general(bm.astype(BF16), xw[:, lanes], (((0,), (0,)), ((), ())),
                                  preferred_element_type=F32)
            st_parts.append(st[:, lanes] * decay_last[:, lanes] + inc)
        sout_ref[b] = jnp.concatenate(st_parts, axis=-1).T
        y = jnp.concatenate(y_parts, axis=-1) + dskx_ref[...] * xs
        y_ref[b] = _gated_group_norm(y, z_ref[b], g_ref[...]).astype(y_ref.dtype)


def _ssd_step(z, xbc, dtx, hist, s0, wts, nseq):
    nb, lc, _ = z.shape
    assert nb % nseq == 0 and lc >= CONV_W - 1
    blk = lambda r, w: pl.BlockSpec((nseq, r, w), lambda i: (i, 0, 0))
    vec = lambda w: _const_spec((1, w))
    sds = jax.ShapeDtypeStruct
    return pl.pallas_call(
        functools.partial(_ssd_step_kernel, nseq=nseq, lc=lc),
        grid=(nb // nseq,),
        in_specs=[blk(lc, SSD_INNER), blk(lc, CONV_DIM), blk(lc, SSD_INNER), blk(CONV_W - 1, CONV_DIM),
                  blk(SSD_INNER, D_STATE), _const_spec((CONV_W, CONV_DIM)), vec(CONV_DIM),
                  vec(SSD_INNER), vec(SSD_INNER), vec(SSD_INNER), vec(SSD_INNER)],
        out_specs=[blk(lc, SSD_INNER), blk(SSD_INNER, D_STATE), blk(CONV_W - 1, CONV_DIM)],
        out_shape=[sds((nb, lc, SSD_INNER), F32), sds((nb, SSD_INNER, D_STATE), F32),
                   sds((nb, CONV_W - 1, CONV_DIM), F32)],
        compiler_params=_params(("arbitrary",)),
        name="ssd_step",
    )(z, xbc, dtx, hist, s0, wts["conv_w"], wts["conv_b"], wts["dtbx"], wts["alogx"], wts["dskx"], wts["g_ssd"])


def _prep_ssd_weights(conv_w, conv_b, dt_bias, a_log, d_skip, g_ssd_norm):
    pad16 = lambda v: jnp.pad(v.reshape(1, SSD_HEADS), ((0, 0), (0, LANES - SSD_HEADS)))
    expand = lambda v: jnp.repeat(v.reshape(1, SSD_HEADS), SSD_HEAD_DIM, axis=1)
    return {"conv_w": conv_w, "conv_b": conv_b.reshape(1, CONV_DIM), "dtb16": pad16(dt_bias), "alog16": pad16(a_log),
            "dtbx": expand(dt_bias), "alogx": expand(a_log), "dskx": expand(d_skip),
            "g_ssd": g_ssd_norm.reshape(1, SSD_INNER)}


PAGES_PER_STEP = 32


def _attn_sample_kernel(pt_ref, qlat_ref, qpe_ref, cnew_ref, knew_ref, wuv_ref, g_ref, ckv_hbm, kpe_hbm,
                        o_ref, m_ref, l_ref, acc_ref, cbuf, kbuf, sems, *, lq, page):
    b = pl.program_id(0)
    j = pl.program_id(1)
    nj = pl.num_programs(1)
    step = b * nj + j
    n_steps = pl.num_programs(0) * nj
    q_lat = qlat_ref[...]
    q_pe = qpe_ref[...]

    def page_copies(stp, slot, fn):
        sb = stp // nj
        sj = stp - sb * nj
        for p in range(PAGES_PER_STEP):
            pg = pt_ref[sb, sj * PAGES_PER_STEP + p]
            rows = pl.ds(p * page, page)
            fn(pltpu.make_async_copy(ckv_hbm.at[pg], cbuf.at[slot, rows], sems.at[slot]), p)
            fn(pltpu.make_async_copy(kpe_hbm.at[pg], kbuf.at[slot, :, rows], sems.at[slot]), p)

    start = lambda c, p: c.start(priority=p % 2)

    @pl.when(step == 0)
    def _():
        page_copies(0, 0, start)

    @pl.when(step + 1 < n_steps)
    def _():
        page_copies(step + 1, (step + 1) % 2, start)

    @pl.when(j == 0)
    def _():
        m_ref[...] = jnp.full(m_ref.shape, -jnp.inf, F32)
        l_ref[...] = jnp.zeros(l_ref.shape, F32)
        acc_ref[...] = jnp.zeros(acc_ref.shape, F32)

    def absorb(s, c):
        m_old = m_ref[...]
        m_new = jnp.maximum(m_old, jnp.max(s, axis=-1, keepdims=True))
        corr = jnp.exp(m_old - m_new)
        p = jnp.exp(s - m_new)
        m_ref[...] = m_new
        l_ref[...] = l_ref[...] * corr + jnp.sum(p, axis=-1, keepdims=True)
        acc_ref[...] = acc_ref[...] * corr + jnp.dot(p.astype(BF16), c, preferred_element_type=F32)

    slot = step % 2
    page_copies(step, slot, lambda c, p: c.wait())
    c = cbuf[slot].astype(BF16)
    kp_t = kbuf[slot].astype(BF16)
    absorb((_nt_dot(q_lat, c) + jnp.dot(q_pe, kp_t, preferred_element_type=F32)) * SM_SCALE, c)

    @pl.when(j == nj - 1)
    def _():
        rows = MLA_HEADS * lq
        c_new = cnew_ref[...].astype(BF16)
        s = (_nt_dot(q_lat, c_new) + _nt_dot(q_pe, knew_ref[...].astype(BF16))) * SM_SCALE
        q_idx = lax.broadcasted_iota(jnp.int32, (rows, lq), 0) % lq
        k_idx = lax.broadcasted_iota(jnp.int32, (rows, lq), 1)
        absorb(jnp.where(q_idx >= k_idx, s, -jnp.inf), c_new)
        o_ref[...] = _attn_out_proj(acc_ref[...] / l_ref[...], wuv_ref, g_ref, lq).astype(o_ref.dtype)


def _attn_sample(page_table, qlat, qpe, c_new, k_new, pool_ckv, pool_kpe, w_uv_r, g_attn):
    nb, rows, _ = qlat.shape
    lq = rows // MLA_HEADS
    n_pages = page_table.shape[1]
    page = pool_ckv.shape[1]
    assert n_pages % PAGES_PER_STEP == 0
    per_b = lambda r, w: pl.BlockSpec((None, r, w), lambda b, j, pt: (b, 0, 0))
    const = lambda shape: pl.BlockSpec(shape, lambda b, j, pt: (0,) * len(shape), pipeline_mode=pl.Buffered(1))
    keys = PAGES_PER_STEP * page
    return pl.pallas_call(
        functools.partial(_attn_sample_kernel, lq=lq, page=page),
        grid_spec=pltpu.PrefetchScalarGridSpec(
            num_scalar_prefetch=1,
            grid=(nb, n_pages // PAGES_PER_STEP),
            in_specs=[per_b(rows, KV_LORA), per_b(rows, QK_ROPE), per_b(lq, KV_LORA), per_b(lq, QK_ROPE),
                      const((MLA_HEADS, KV_LORA, V_DIM)), const((1, MLA_INNER)),
                      pl.BlockSpec(memory_space=pl.ANY), pl.BlockSpec(memory_space=pl.ANY)],
            out_specs=pl.BlockSpec((None, lq, MLA_INNER), lambda b, j, pt: (b, 0, 0)),
            scratch_shapes=[pltpu.VMEM((rows, 1), F32), pltpu.VMEM((rows, 1), F32), pltpu.VMEM((rows, KV_LORA), F32),
                            pltpu.VMEM((2, keys, KV_LORA), F32), pltpu.VMEM((2, QK_ROPE, keys), F32),
                            pltpu.SemaphoreType.DMA((2,))]),
        out_shape=jax.ShapeDtypeStruct((nb, lq, MLA_INNER), F32),
        compiler_params=_params(("arbitrary", "arbitrary")),
        name="attn_sample",
    )(page_table, qlat, qpe, c_new, k_new, w_uv_r, g_attn, pool_ckv, pool_kpe)


def _merge_router_kernel(ha_ref, ya_ref, oa_ref, hb_ref, yb_ref, ob_ref, wa_ref, wb_ref, gffn_ref, wr_ref, br_ref,
                         h2_ref, xn_ref, topi_ref, gate_ref, rank_ref, cnt_ref, carry_ref, *, tm, steps_a):
    i = pl.program_id(0)

    @pl.when(i == 0)
    def _():
        carry_ref[...] = jnp.zeros(carry_ref.shape, F32)

    first = i < steps_a
    pick = lambda a_ref, b_ref, dt: jnp.where(first, a_ref[...].astype(dt), b_ref[...].astype(dt))
    mix = (jnp.dot(pick(ya_ref, yb_ref, BF16), wa_ref[...], preferred_element_type=F32)
           + jnp.dot(pick(oa_ref, ob_ref, BF16), wb_ref[...], preferred_element_type=F32))
    h2 = pick(ha_ref, hb_ref, F32) + mix
    h2_ref[...] = h2
    xn32 = _rms(h2, gffn_ref[...])
    xn4 = _tile_view(xn_ref)
    for ct in range(ROW_TILES):
        xn4[:, ct] = xn32[:, ct * LANES:(ct + 1) * LANES].reshape(tm // SUBLANES, SUBLANES, LANES)
    xn = xn32.astype(BF16)
    lane = lax.broadcasted_iota(jnp.int32, (tm, LANES), 1)
    logits = jnp.dot(xn, wr_ref[...], preferred_element_type=F32) + br_ref[...]
    logits = jnp.where(lane < N_EXPERTS, logits, -jnp.inf)
    vals, idxs = [], []
    for _ in range(TOP_K):
        v = jnp.max(logits, axis=-1, keepdims=True)
        idx = jnp.min(jnp.where(logits == v, lane, LANES), axis=-1, keepdims=True)
        vals.append(v)
        idxs.append(idx)
        logits = jnp.where(lane == idx, -jnp.inf, logits)
    exps = [jnp.exp(v - vals[0]) for v in vals]
    denom = exps[0]
    for e in exps[1:]:
        denom = denom + e
    hit = [lane == idx for idx in idxs]
    onehot = jnp.zeros((tm, LANES), F32)
    for hk in hit:
        onehot = onehot + jnp.where(hk, 1.0, 0.0)
    r_i = lax.broadcasted_iota(jnp.int32, (tm, tm), 0)
    c_j = lax.broadcasted_iota(jnp.int32, (tm, tm), 1)
    strict_tril = jnp.where(r_i > c_j, 1.0, 0.0).astype(BF16)
    before = jnp.dot(strict_tril, onehot.astype(BF16), preferred_element_type=F32) + carry_ref[...]
    carry_ref[...] = carry_ref[...] + jnp.sum(onehot, axis=0, keepdims=True)
    topi = jnp.zeros((tm, LANES), jnp.int32)
    gate = jnp.zeros((tm, LANES), F32)
    rank = jnp.zeros((tm, LANES), jnp.int32)
    for k in range(TOP_K):
        rk = jnp.sum(jnp.where(hit[k], before, 0.0), axis=-1, keepdims=True).astype(jnp.int32)
        topi = jnp.where(lane == k, idxs[k], topi)
        gate = jnp.where(lane == k, exps[k] / denom, gate)
        rank = jnp.where(lane == k, rk, rank)
    topi_ref[...] = topi
    gate_ref[...] = gate
    rank_ref[...] = rank
    cnt_ref[...] = carry_ref[...]


def _merge_router(group_a, group_b, wts, tm):
    rows_a, rows_b = group_a[0].shape[0], group_b[0].shape[0]
    assert rows_a % tm == 0 and rows_b % tm == 0
    steps_a, steps_b = rows_a // tm, rows_b // tm
    rows = rows_a + rows_b
    spec_a = lambda w: pl.BlockSpec((tm, w), lambda i: (jnp.minimum(i, steps_a - 1), 0))
    spec_b = lambda w: pl.BlockSpec((tm, w), lambda i: (jnp.maximum(i - steps_a, 0), 0))
    row = lambda w: pl.BlockSpec((tm, w), lambda i: (i, 0))
    widths = (D_MODEL, SSD_INNER, MLA_INNER)
    sds = jax.ShapeDtypeStruct
    return pl.pallas_call(
        functools.partial(_merge_router_kernel, tm=tm, steps_a=steps_a),
        grid=(steps_a + steps_b,),
        in_specs=[spec_a(w) for w in widths] + [spec_b(w) for w in widths]
                 + [_const_spec((SSD_INNER, D_MODEL)), _const_spec((MLA_INNER, D_MODEL)), _const_spec((1, D_MODEL)),
                    _const_spec((D_MODEL, LANES)), _const_spec((1, LANES))],
        out_specs=[row(D_MODEL), _gatherable_rows_spec(tm, lambda i: i), row(LANES), row(LANES), row(LANES),
                   pl.BlockSpec((1, LANES), lambda i: (0, 0))],
        out_shape=[sds((rows, D_MODEL), F32), _gatherable_rows_shape(rows), sds((rows, LANES), jnp.int32),
                   sds((rows, LANES), F32), sds((rows, LANES), jnp.int32), sds((1, LANES), F32)],
        scratch_shapes=[pltpu.VMEM((1, LANES), F32)],
        compiler_params=_params(("arbitrary",)),
        name="merge_router",
    )(*group_a, *group_b, wts["w_out_a"], wts["w_out_b"], wts["g_ffn"], wts["w_router"], wts["b_router"])


MOE_TM = 1152
MOE_TS = 384
MOE_TF = 256
MOE_GATHER_ROWS = MOE_TM // (D_FF // MOE_TF)
COMBINE_TM = 256


def _moe_kernel(te_ref, nv_ref, src_ref, xn_hbm, wg_ref, wu_ref, wd_ref, bg_ref, bu_ref, bd_ref, y_ref,
                xland, xbf, wg_s, wu_s, wd_s, sem):
    del te_ref
    i = pl.program_id(0)
    f = pl.program_id(1)
    n_tiles = pl.num_programs(0)
    last_f = pl.num_programs(1) - 1
    nv = nv_ref[i]
    n_sub_max = MOE_TM // MOE_TS
    sub_tiles = MOE_TS // SUBLANES
    y4 = _tile_view(y_ref)
    land4 = _tile_view(xland)

    def row_copy(src_row, dst_row):
        return _gather_row_copy(xn_hbm, src_row, xland, dst_row, sem)

    def wait_tile_rows():
        for r in range(MOE_TM):
            row_copy(0, r).wait()

    @pl.when((i == 0) & (f == 0))
    def _():
        lax.fori_loop(0, MOE_TM, lambda r, c: (row_copy(src_ref[r], r).start(), c)[1], 0)

    requested = (i == 0) | (nv_ref[jnp.maximum(i - 1, 0)] > 0)

    @pl.when((f == 0) & requested)
    def _():
        wait_tile_rows()
        for s in range(n_sub_max):
            @pl.when(s * MOE_TS < nv)
            def _():
                for ct in range(ROW_TILES):
                    piece = land4[s * sub_tiles:(s + 1) * sub_tiles, ct].reshape(MOE_TS, LANES)
                    xbf[s * MOE_TS:(s + 1) * MOE_TS, ct * LANES:(ct + 1) * LANES] = piece.astype(BF16)

    for s in range(n_sub_max):
        tiles = slice(s * sub_tiles, (s + 1) * sub_tiles)

        @pl.when((f == 0) & (s * MOE_TS < nv))
        def _():
            for ct in range(ROW_TILES):
                y4[tiles, ct] = jnp.broadcast_to(bd_ref[:, ct * LANES:(ct + 1) * LANES], (sub_tiles, SUBLANES, LANES))

        @pl.when((f == 0) & (s * MOE_TS >= nv))
        def _():
            y4[tiles] = jnp.zeros((sub_tiles, ROW_TILES, SUBLANES, LANES), F32)

    def tile_step(n_sub):
        wg_s[...] = wg_ref[...].astype(BF16)
        wu_s[...] = wu_ref[...].astype(BF16)
        wd_s[...] = wd_ref[...].astype(BF16)
        for r in range(MOE_GATHER_ROWS):
            row = f * MOE_GATHER_ROWS + r
            row_copy(src_ref[(i + 1) * MOE_TM + row], row).start(priority=1)
        for s in range(n_sub):
            x = xbf[s * MOE_TS:(s + 1) * MOE_TS, :]
            g = jnp.minimum(jnp.dot(x, wg_s[...], preferred_element_type=F32) + bg_ref[...], SWIGLU_LIMIT)
            u = jnp.clip(jnp.dot(x, wu_s[...], preferred_element_type=F32) + bu_ref[...],
                         -SWIGLU_LIMIT, SWIGLU_LIMIT)
            hdn = ((u + 1.0) * (g * jax.nn.sigmoid(SWIGLU_ALPHA * g))).astype(BF16)
            part = jnp.dot(hdn, wd_s[...], preferred_element_type=F32)
            tiles = slice(s * sub_tiles, (s + 1) * sub_tiles)
            for ct in range(ROW_TILES):
                y4[tiles, ct] = y4[tiles, ct] + part[:, ct * LANES:(ct + 1) * LANES].reshape(
                    sub_tiles, SUBLANES, LANES)

    n_sub_valid = (nv + MOE_TS - 1) // MOE_TS
    for k in range(1, n_sub_max + 1):
        pl.when(n_sub_valid == k)(functools.partial(tile_step, k))

    @pl.when((i == n_tiles - 1) & (f == last_f) & (nv > 0))
    def _():
        wait_tile_rows()


def _moe_experts(tile_expert, tile_nvalid, src_rows, xn_rows, w_gate, b_gate, w_up, b_up, w_down, b_down):
    n_tiles = src_rows.shape[0] // MOE_TM - 1
    n_f = D_FF // MOE_TF
    assert MOE_GATHER_ROWS * n_f == MOE_TM
    fidx = lambda i, f, nv: jnp.where(nv[i] > 0, f, n_f - 1)
    wspec = lambda shape, imap: pl.BlockSpec(shape, lambda i, f, te, nv, src: imap(te[i], fidx(i, f, nv)))
    return pl.pallas_call(
        _moe_kernel,
        grid_spec=pltpu.PrefetchScalarGridSpec(
            num_scalar_prefetch=3, grid=(n_tiles, n_f),
            in_specs=[pl.BlockSpec(memory_space=pl.ANY),
                      wspec((None, D_MODEL, MOE_TF), lambda e, f: (e, 0, f)),
                      wspec((None, D_MODEL, MOE_TF), lambda e, f: (e, 0, f)),
                      wspec((None, MOE_TF, D_MODEL), lambda e, f: (e, f, 0)),
                      wspec((None, 1, MOE_TF), lambda e, f: (e, 0, f)),
                      wspec((None, 1, MOE_TF), lambda e, f: (e, 0, f)),
                      wspec((None, 1, D_MODEL), lambda e, f: (e, 0, 0))],
            out_specs=_gatherable_rows_spec(MOE_TM, lambda i, f, te, nv, src: i),
            scratch_shapes=[_row_landing_scratch(MOE_TM), pltpu.VMEM((MOE_TM, D_MODEL), BF16),
                            pltpu.VMEM((D_MODEL, MOE_TF), BF16), pltpu.VMEM((D_MODEL, MOE_TF), BF16),
                            pltpu.VMEM((MOE_TF, D_MODEL), BF16), pltpu.SemaphoreType.DMA(())]),
        out_shape=_gatherable_rows_shape(n_tiles * MOE_TM),
        compiler_params=_params(("arbitrary", "arbitrary")),
        name="moe_experts",
    )(tile_expert, tile_nvalid, src_rows, xn_rows, w_gate, w_up, w_down,
      b_gate.reshape(N_EXPERTS, 1, D_FF), b_up.reshape(N_EXPERTS, 1, D_FF), b_down.reshape(N_EXPERTS, 1, D_MODEL))


def _combine_kernel(pos_ref, h2_ref, gate_ref, y_hbm, gfin_ref, out_ref, *scratch, tm, row_offset):
    bufs, sems = scratch[:2 * TOP_K], scratch[2 * TOP_K]
    i = pl.program_id(0)
    n = pl.num_programs(0)

    def row_copy(src_row, slot, k, t):
        return _gather_row_copy(y_hbm, src_row, bufs[slot * TOP_K + k], t, sems.at[slot])

    def start_copies(step, slot):
        def body(t, carry):
            for k in range(TOP_K):
                row_copy(pos_ref[(row_offset + step * tm + t) * TOP_K + k], slot, k, t).start(priority=k % 2)
            return carry
        lax.fori_loop(0, tm, body, 0, unroll=2)

    def wait_copies(slot):
        for t in range(tm):
            for k in range(TOP_K):
                row_copy(0, slot, k, t).wait()

    def for_slot(slot_value, fn):
        for slot in range(2):
            pl.when(slot_value == slot)(functools.partial(fn, slot))

    @pl.when(i == 0)
    def _():
        start_copies(0, 0)

    @pl.when(i + 1 < n)
    def _():
        for_slot((i + 1) % 2, lambda slot: start_copies(i + 1, slot))

    def finish(slot):
        wait_copies(slot)
        gate = gate_ref[...]
        gates = [jnp.broadcast_to(gate[:, k:k + 1], (tm, LANES)) for k in range(TOP_K)]
        views = [_tile_view(bufs[slot * TOP_K + k]) for k in range(TOP_K)]
        parts = []
        for ct in range(ROW_TILES):
            acc = h2_ref[:, ct * LANES:(ct + 1) * LANES]
            for k in range(TOP_K):
                acc = acc + gates[k] * views[k][:, ct].reshape(tm, LANES)
            parts.append(acc)
        out_ref[...] = _rms(jnp.concatenate(parts, axis=-1), gfin_ref[...])

    for_slot(i % 2, finish)


def _combine(pos_flat, h2, gate, y_rows, g_final, row_offset, rows):
    tm = COMBINE_TM
    assert rows % tm == 0 and row_offset % tm == 0
    off = row_offset // tm
    return pl.pallas_call(
        functools.partial(_combine_kernel, tm=tm, row_offset=row_offset),
        grid_spec=pltpu.PrefetchScalarGridSpec(
            num_scalar_prefetch=1, grid=(rows // tm,),
            in_specs=[pl.BlockSpec((tm, D_MODEL), lambda i, pos: (i + off, 0)),
                      pl.BlockSpec((tm, LANES), lambda i, pos: (i + off, 0)),
                      pl.BlockSpec(memory_space=pl.ANY),
                      pl.BlockSpec((1, D_MODEL), lambda i, pos: (0, 0))],
            out_specs=pl.BlockSpec((tm, D_MODEL), lambda i, pos: (i, 0)),
            scratch_shapes=[_row_landing_scratch(tm) for _ in range(2 * TOP_K)] + [pltpu.SemaphoreType.DMA((2,))]),
        out_shape=jax.ShapeDtypeStruct((rows, D_MODEL), F32),
        compiler_params=_params(("arbitrary",)),
        name="moe_combine",
    )(pos_flat, h2, gate, y_rows, g_final)


def _routing_tables(topi, rank, counts, n_rows):
    counts = counts[0, :N_EXPERTS].astype(jnp.int32)
    tiles_per = (counts + MOE_TM - 1) // MOE_TM
    tile_end = jnp.cumsum(tiles_per)
    tile_start = tile_end - tiles_per
    n_tiles_max = (n_rows * TOP_K + N_EXPERTS * (MOE_TM - 1)) // MOE_TM
    pos = tile_start[topi] * MOE_TM + rank
    src = jnp.zeros(((n_tiles_max + 1) * MOE_TM,), jnp.int32).at[pos.reshape(-1)].set(
        jnp.repeat(jnp.arange(n_rows, dtype=jnp.int32), TOP_K))
    tile_id = jnp.arange(n_tiles_max, dtype=jnp.int32)
    n_used = tile_end[-1]
    last = jnp.maximum(n_used - 1, 0)
    tile_row = jnp.minimum(tile_id, last)
    owner = jnp.sum((tile_end[None, :] <= tile_row[:, None]).astype(jnp.int32), axis=1)
    tile_expert = jnp.minimum(owner, N_EXPERTS - 1)
    in_tile = tile_row - tile_start[tile_expert]
    nvalid = jnp.clip(counts[tile_expert] - in_tile * MOE_TM, 0, MOE_TM)
    nvalid = jnp.where(tile_id < n_used, nvalid, 0).astype(jnp.int32)
    return pos.reshape(-1).astype(jnp.int32), src, tile_expert, nvalid


def kernel(x_prompt, x_sample, cache_ckv, cache_kpe, state_ssm, state_conv, page_table, meta_tokens, g_mix_norm, w_in, g_q_norm, w_q_up, g_kv_norm, w_uk, w_uv, conv_w, conv_b, dt_bias, a_log, d_skip, g_ssd_norm, g_attn_out, w_out, g_ffn_norm, w_router, b_router, w_gate, b_gate, w_up, b_up, w_down, b_down, g_final_norm):
    assert w_in.shape[0] == 1, "single-layer model"
    bp, seq = x_prompt.shape[:2]
    bs, dseq = x_sample.shape[:2]
    rows_p, rows_s = bp * seq, bs * dseq
    past_len = page_table.shape[1] * cache_ckv.shape[2]
    wm = _prep_mixer_weights(g_mix_norm[0], w_in[0], g_q_norm[0], w_q_up[0], g_kv_norm[0], w_uk[0])
    ws = _prep_ssd_weights(conv_w[0], conv_b[0], dt_bias[0], a_log[0], d_skip[0], g_ssd_norm[0])
    w_uv_r = jnp.transpose(w_uv[0], (1, 0, 2)).astype(BF16)
    g_attn = g_attn_out[0].reshape(1, MLA_INNER)
    wr = {"w_out_a": w_out[0][:SSD_INNER].astype(BF16), "w_out_b": w_out[0][SSD_INNER:].astype(BF16),
          "g_ffn": g_ffn_norm[0].reshape(1, D_MODEL),
          "w_router": jnp.pad(w_router[0], ((0, 0), (0, LANES - N_EXPERTS))).astype(BF16),
          "b_router": jnp.pad(b_router[0], (0, LANES - N_EXPERTS)).reshape(1, LANES)}

    tm = 256
    a_m = _inproj(meta_tokens, *_rope_tables(jnp.arange(N_META)), wm, N_META)
    a_p = _inproj(x_prompt.reshape(rows_p, D_MODEL), *_rope_tables(N_META + jnp.arange(seq)), wm, tm)
    cs_s = [jnp.tile(t, (tm // dseq, 1)) for t in _rope_tables(past_len + jnp.arange(dseq))]
    a_s = _inproj(x_sample.reshape(rows_s, D_MODEL), *cs_s, wm, tm)
    qlat_m, qpe_m, ckv_m, ckvb_m, kpe_m, kpeb_m, z_m, xbc_m, dt16_m, dtx_m = a_m
    qlat_p, qpe_p, ckv_p, ckvb_p, kpe_p, kpeb_p, z_p, xbc_p, dt16_p, dtx_p = a_p
    qlat_s, qpe_s, ckv_s, ckvb_s, kpe_s, kpeb_s, z_s, xbc_s, dt16_s, dtx_s = a_s
    del qlat_m, qpe_m, ckvb_s, kpeb_s, dt16_s

    heads_first = lambda q, b, t: q.reshape(MLA_HEADS, b, t, q.shape[-1])
    o_p = _attn_prompt(heads_first(qlat_p, bp, seq), heads_first(qpe_p, bp, seq),
                       ckvb_p.reshape(bp, seq, KV_LORA), kpeb_p.reshape(bp, seq, QK_ROPE),
                       ckvb_m, kpeb_m, w_uv_r, g_attn, 256)
    per_seq = lambda q: jnp.transpose(heads_first(q, bs, dseq), (1, 0, 2, 3)).reshape(bs, MLA_HEADS * dseq, q.shape[-1])
    o_s = _attn_sample(page_table, per_seq(qlat_s), per_seq(qpe_s), ckv_s.reshape(bs, dseq, KV_LORA),
                       kpe_s.reshape(bs, dseq, QK_ROPE), cache_ckv[0], jnp.swapaxes(cache_kpe[0], 1, 2),
                       w_uv_r, g_attn)

    pad_meta = lambda v: jnp.pad(v, ((0, CHUNK - N_META), (0, 0)))[None]
    _, ssm_m, conv_m = _ssd_chunked(pad_meta(z_m), pad_meta(xbc_m), pad_meta(dt16_m), pad_meta(dtx_m),
                                    jnp.zeros((1, CONV_W - 1, CONV_DIM), F32),
                                    jnp.zeros((1, SSD_INNER, D_STATE), F32), ws, CHUNK, N_META, True)
    per_b = lambda v, b, t: v.reshape(b, t, v.shape[-1])
    y_p, ssm_p, conv_p = _ssd_chunked(per_b(z_p, bp, seq), per_b(xbc_p, bp, seq), per_b(dt16_p, bp, seq),
                                      per_b(dtx_p, bp, seq), conv_m, ssm_m, ws, CHUNK, CHUNK, True)
    y_s, ssm_s, conv_s = _ssd_step(per_b(z_s, bs, dseq), per_b(xbc_s, bs, dseq), per_b(dtx_s, bs, dseq),
                                   state_conv[0], state_ssm[0].reshape(bs, SSD_INNER, D_STATE), ws, 8)

    total = rows_p + rows_s
    h2, xn_rows, topi, gate, rank, counts = _merge_router(
        (x_prompt.reshape(rows_p, D_MODEL), y_p.reshape(rows_p, SSD_INNER), o_p.reshape(rows_p, MLA_INNER)),
        (x_sample.reshape(rows_s, D_MODEL), y_s.reshape(rows_s, SSD_INNER), o_s.reshape(rows_s, MLA_INNER)),
        wr, tm)

    pos, src, tile_expert, tile_nvalid = _routing_tables(topi[:, :TOP_K], rank[:, :TOP_K], counts, total)
    y_rows = _moe_experts(tile_expert, tile_nvalid, src, xn_rows, w_gate[0], b_gate[0], w_up[0],
                          b_up[0], w_down[0], b_down[0])
    g_fin = g_final_norm.reshape(1, D_MODEL)
    y_prompt = _combine(pos, h2, gate, y_rows, g_fin, 0, rows_p).reshape(bp, seq, D_MODEL)
    y_sample = _combine(pos, h2, gate, y_rows, g_fin, rows_p, rows_s).reshape(bs, dseq, D_MODEL)

    with_meta = lambda m, p, w: jnp.concatenate(
        [jnp.broadcast_to(m[None], (bp, N_META, w)), p.reshape(bp, seq, w)], axis=1)[None]
    return (y_prompt, y_sample,
            with_meta(ckv_m, ckv_p, KV_LORA), with_meta(kpe_m, kpe_p, QK_ROPE),
            ssm_p.reshape(1, bp, SSD_HEADS, SSD_HEAD_DIM, D_STATE), conv_p[None],
            ckv_s.reshape(1, bs, dseq, KV_LORA), kpe_s.reshape(1, bs, dseq, QK_ROPE),
            ssm_s.reshape(1, bs, SSD_HEADS, SSD_HEAD_DIM, D_STATE), conv_s[None])
```

```python
import functools
import math

import jax
import jax.numpy as jnp
from jax import lax
from jax.experimental import pallas as pl
from jax.experimental.pallas import tpu as pltpu

F32 = jnp.float32
BF16 = jnp.bfloat16

D_MODEL = 2048
N_META = 16
EPS = 1e-6
SSD_HEADS = 16
SSD_HEAD_DIM = 64
SSD_INNER = SSD_HEADS * SSD_HEAD_DIM
SSD_GROUPS = 2
SSD_HPG = SSD_HEADS // SSD_GROUPS
D_STATE = 128
CONV_W = 4
CONV_DIM = SSD_INNER + 2 * SSD_GROUPS * D_STATE
CHUNK = 128
MLA_HEADS = 8
Q_LORA = 512
KV_LORA = 512
QK_NOPE = 128
QK_ROPE = 64
V_DIM = 128
MLA_INNER = MLA_HEADS * V_DIM
ROPE_THETA = 10000.0
SM_SCALE = (QK_NOPE + QK_ROPE) ** -0.5
N_EXPERTS = 32
TOP_K = 4
D_FF = 2048
SWIGLU_LIMIT = 7.0
SWIGLU_ALPHA = 1.702

LANES = 128
SUBLANES = 8
ROW_TILES = D_MODEL // LANES
VMEM_LIMIT_BYTES = 56 * 1024 * 1024

U_QA = 0
U_KV = U_QA + Q_LORA
U_Z = U_KV + KV_LORA
U_XBC = U_Z + SSD_INNER
U_KPE = U_XBC + CONV_DIM
U_TAIL = U_KPE + 2 * QK_ROPE
U_DTX = U_TAIL + LANES
U_WIDTH = U_DTX + SSD_INNER
Q_NOPE_W = MLA_HEADS * QK_NOPE
Q_ROPE_W = MLA_HEADS * QK_ROPE
Q_UP_WIDTH = Q_NOPE_W + 2 * Q_ROPE_W


def _params(semantics):
    return pltpu.CompilerParams(dimension_semantics=semantics, vmem_limit_bytes=VMEM_LIMIT_BYTES)


def _const_spec(shape):
    nd = len(shape)
    return pl.BlockSpec(shape, lambda *_: (0,) * nd, pipeline_mode=pl.Buffered(1))


def _rms(x, g):
    return x * lax.rsqrt(jnp.mean(x * x, axis=-1, keepdims=True) + EPS) * g


def _gatherable_rows_shape(rows):
    return jax.ShapeDtypeStruct((rows // SUBLANES, ROW_TILES, SUBLANES, 1, LANES), F32)


def _gatherable_rows_spec(tm, row_block):
    return pl.BlockSpec((tm // SUBLANES, ROW_TILES, SUBLANES, 1, LANES), lambda *a: (row_block(*a), 0, 0, 0, 0))


def _tile_view(ref):
    return ref.reshape(ref.shape[0], ROW_TILES, SUBLANES, LANES)


def _row_landing_scratch(rows):
    return pltpu.VMEM((rows // SUBLANES, ROW_TILES, SUBLANES, 1, LANES), F32)


def _gather_row_copy(src, src_row, dst, dst_row, sem):
    def split(r):
        if isinstance(r, int):
            return r // SUBLANES, r % SUBLANES
        return lax.shift_right_logical(r, 3), lax.bitwise_and(r, SUBLANES - 1)

    s_rt, s_sl = split(src_row)
    d_rt, d_sl = split(dst_row)
    return pltpu.make_async_copy(src.at[s_rt, :, s_sl], dst.at[d_rt, :, d_sl], sem)


def _rot_half_cols(w):
    half = QK_ROPE // 2
    return jnp.concatenate([-w[..., half:], w[..., :half]], axis=-1)


def _inproj_kernel(x_ref, gmix_ref, win_ref, gq_ref, wq_ref, wuk_ref, gkv_ref, cs_q_ref, cs_k_ref,
                   qlat_ref, qpe_ref, ckv_ref, ckvb_ref, kpe_ref, kpeb_ref, z_ref, xbc_ref, tail_ref, dtx_ref):
    xn = _rms(x_ref[...], gmix_ref[...]).astype(BF16)
    u = jnp.dot(xn, win_ref[...], preferred_element_type=F32)
    z_ref[...] = u[:, U_Z:U_Z + SSD_INNER]
    xbc_ref[...] = u[:, U_XBC:U_XBC + CONV_DIM]
    tail_ref[...] = u[:, U_TAIL:U_TAIL + LANES]
    dtx_ref[...] = u[:, U_DTX:U_DTX + SSD_INNER]
    ckv = _rms(u[:, U_KV:U_KV + KV_LORA], gkv_ref[...])
    ckv_ref[...] = ckv
    ckvb_ref[...] = ckv.astype(BF16)
    kprod = u[:, U_KPE:U_KPE + 2 * QK_ROPE] * cs_k_ref[...]
    kpe = kprod[:, :QK_ROPE] + kprod[:, QK_ROPE:]
    kpe_ref[...] = kpe
    kpeb_ref[...] = kpe.astype(BF16)
    qn = _rms(u[:, U_QA:U_QA + Q_LORA], gq_ref[...]).astype(BF16)
    q = jnp.dot(qn, wq_ref[...], preferred_element_type=F32)
    cs_q = cs_q_ref[...]
    qpe = (q[:, Q_NOPE_W:Q_NOPE_W + Q_ROPE_W] * cs_q[:, :Q_ROPE_W]
           + q[:, Q_NOPE_W + Q_ROPE_W:] * cs_q[:, Q_ROPE_W:]).astype(BF16)
    for h in range(MLA_HEADS):
        qpe_ref[h] = qpe[:, h * QK_ROPE:(h + 1) * QK_ROPE]
        qn_h = q[:, h * QK_NOPE:(h + 1) * QK_NOPE].astype(BF16)
        qlat_ref[h] = jnp.dot(qn_h, wuk_ref[h], preferred_element_type=F32).astype(BF16)


def _inproj(x, cs_q, cs_k, wts, tm):
    rows = x.shape[0]
    pos_rows = cs_q.shape[0]
    assert rows % tm == 0 and pos_rows % tm == 0
    n_pos = pos_rows // tm
    row = lambda w: pl.BlockSpec((tm, w), lambda i: (i, 0))
    pos = lambda w: pl.BlockSpec((tm, w), lambda i: (i % n_pos, 0))
    head = lambda w: pl.BlockSpec((MLA_HEADS, tm, w), lambda i: (0, i, 0))
    sds = jax.ShapeDtypeStruct
    return pl.pallas_call(
        _inproj_kernel,
        grid=(rows // tm,),
        in_specs=[row(D_MODEL), _const_spec((1, D_MODEL)), _const_spec((D_MODEL, U_WIDTH)),
                  _const_spec((1, Q_LORA)), _const_spec((Q_LORA, Q_UP_WIDTH)),
                  _const_spec((MLA_HEADS, QK_NOPE, KV_LORA)), _const_spec((1, KV_LORA)),
                  pos(2 * Q_ROPE_W), pos(2 * QK_ROPE)],
        out_specs=[head(KV_LORA), head(QK_ROPE), row(KV_LORA), row(KV_LORA), row(QK_ROPE), row(QK_ROPE),
                   row(SSD_INNER), row(CONV_DIM), row(LANES), row(SSD_INNER)],
        out_shape=[sds((MLA_HEADS, rows, KV_LORA), BF16), sds((MLA_HEADS, rows, QK_ROPE), BF16),
                   sds((rows, KV_LORA), F32), sds((rows, KV_LORA), BF16),
                   sds((rows, QK_ROPE), F32), sds((rows, QK_ROPE), BF16),
                   sds((rows, SSD_INNER), F32), sds((rows, CONV_DIM), F32), sds((rows, LANES), F32),
                   sds((rows, SSD_INNER), F32)],
        compiler_params=_params(("arbitrary",)),
        name="inproj",
    )(x, wts["g_mix"], wts["w_in"], wts["g_q"], wts["w_q"], wts["w_uk"], wts["g_kv"], cs_q, cs_k)


def _rope_tables(pos):
    inv_freq = ROPE_THETA ** (-jnp.arange(0, QK_ROPE, 2, dtype=F32) / QK_ROPE)
    ang = pos.astype(F32)[:, None] * inv_freq[None, :]
    cos = jnp.tile(jnp.cos(ang), (1, 2))
    sin = jnp.tile(jnp.sin(ang), (1, 2))
    cs_k = jnp.concatenate([cos, sin], axis=-1)
    cs_q = jnp.concatenate([jnp.tile(cos, (1, MLA_HEADS)), jnp.tile(sin, (1, MLA_HEADS))], axis=-1)
    return cs_q, cs_k


def _prep_mixer_weights(g_mix_norm, w_in, g_q_norm, w_q_up, g_kv_norm, w_uk):
    s = (0, Q_LORA, Q_LORA + KV_LORA, Q_LORA + KV_LORA + QK_ROPE,
         Q_LORA + KV_LORA + QK_ROPE + SSD_INNER, Q_LORA + KV_LORA + QK_ROPE + SSD_INNER + CONV_DIM)
    w_qa, w_kv, w_kpe, w_z, w_xbc, w_dt = (w_in[:, s[i]:(s[i + 1] if i + 1 < len(s) else None)] for i in range(6))
    pad = jnp.zeros((D_MODEL, LANES - SSD_HEADS), w_in.dtype)
    w_in_r = jnp.concatenate([w_qa, w_kv, w_z, w_xbc, w_kpe, _rot_half_cols(w_kpe), w_dt, pad,
                              jnp.repeat(w_dt, SSD_HEAD_DIM, axis=1)], axis=1)
    wq = w_q_up.reshape(Q_LORA, MLA_HEADS, QK_NOPE + QK_ROPE)
    wq_nope = wq[:, :, :QK_NOPE].reshape(Q_LORA, Q_NOPE_W)
    wq_rope = wq[:, :, QK_NOPE:]
    w_q_r = jnp.concatenate([wq_nope, wq_rope.reshape(Q_LORA, Q_ROPE_W),
                             _rot_half_cols(wq_rope).reshape(Q_LORA, Q_ROPE_W)], axis=1)
    return {
        "g_mix": g_mix_norm.reshape(1, D_MODEL), "w_in": w_in_r.astype(BF16),
        "g_q": g_q_norm.reshape(1, Q_LORA), "w_q": w_q_r.astype(BF16),
        "w_uk": jnp.transpose(w_uk, (1, 2, 0)).astype(BF16),
        "g_kv": g_kv_norm.reshape(1, KV_LORA),
    }


def _nt_dot(a, b):
    return lax.dot_general(a, b, (((1,), (1,)), ((), ())), preferred_element_type=F32)


def _attn_out_proj(o_lat, wuv_ref, g_ref, rows):
    o_b = o_lat.astype(BF16)
    o = jnp.concatenate([jnp.dot(o_b[h * rows:(h + 1) * rows], wuv_ref[h], preferred_element_type=F32)
                         for h in range(MLA_HEADS)], axis=-1)
    return _rms(o, g_ref[...])


def _attn_prompt_kernel(qlat_ref, qpe_ref, ckv_ref, kpe_ref, ckvm_ref, kpem_ref, wuv_ref, g_ref,
                        o_ref, m_ref, l_ref, acc_ref, s_ref, *, bq):
    qi = pl.program_id(1)
    rows = MLA_HEADS * bq
    q_lat = qlat_ref[...].reshape(rows, KV_LORA)
    q_pe = qpe_ref[...].reshape(rows, QK_ROPE)

    def scores(c, kp):
        return (_nt_dot(q_lat, c) + _nt_dot(q_pe, kp)) * SM_SCALE

    cm = ckvm_ref[...]
    s = scores(cm, kpem_ref[...])
    m0 = jnp.max(s, axis=-1, keepdims=True)
    p = jnp.exp(s - m0)
    m_ref[...] = m0
    l_ref[...] = jnp.sum(p, axis=-1, keepdims=True)
    acc_ref[...] = jnp.dot(p.astype(BF16), cm, preferred_element_type=F32)

    def block_scores(j):
        k0 = pl.multiple_of(j * bq, bq)
        return scores(ckv_ref[pl.ds(k0, bq), :], kpe_ref[pl.ds(k0, bq), :])

    def absorb(s, j, masked):
        if masked:
            causal = (lax.broadcasted_iota(jnp.int32, (bq, bq), 0) >= lax.broadcasted_iota(jnp.int32, (bq, bq), 1))
            s = jnp.where(causal[None], s.reshape(MLA_HEADS, bq, bq), -jnp.inf).reshape(rows, bq)
        m_old = m_ref[...]
        m_new = jnp.maximum(m_old, jnp.max(s, axis=-1, keepdims=True))
        corr = jnp.exp(m_old - m_new)
        p = jnp.exp(s - m_new)
        m_ref[...] = m_new
        l_ref[...] = l_ref[...] * corr + jnp.sum(p, axis=-1, keepdims=True)
        c = ckv_ref[pl.ds(pl.multiple_of(j * bq, bq), bq), :]
        acc_ref[...] = acc_ref[...] * corr + jnp.dot(p.astype(BF16), c, preferred_element_type=F32)

    s_ref[...] = block_scores(0)

    def full_block(j, carry):
        s = s_ref[...]
        s_next = block_scores(j + 1)
        absorb(s, j, False)
        s_ref[...] = s_next
        return carry

    lax.fori_loop(0, qi, full_block, 0)
    absorb(s_ref[...], qi, True)
    o_ref[...] = _attn_out_proj(acc_ref[...] / l_ref[...], wuv_ref, g_ref, bq).astype(o_ref.dtype)


def _attn_prompt(qlat, qpe, ckvb, kpeb, ckvb_meta, kpeb_meta, w_uv_r, g_attn, bq):
    _, nb, seq, _ = qlat.shape
    assert seq % bq == 0
    rows = MLA_HEADS * bq
    return pl.pallas_call(
        functools.partial(_attn_prompt_kernel, bq=bq),
        grid=(nb, seq // bq),
        in_specs=[pl.BlockSpec((MLA_HEADS, None, bq, KV_LORA), lambda b, i: (0, b, i, 0)),
                  pl.BlockSpec((MLA_HEADS, None, bq, QK_ROPE), lambda b, i: (0, b, i, 0)),
                  pl.BlockSpec((None, seq, KV_LORA), lambda b, i: (b, 0, 0)),
                  pl.BlockSpec((None, seq, QK_ROPE), lambda b, i: (b, 0, 0)),
                  _const_spec((N_META, KV_LORA)), _const_spec((N_META, QK_ROPE)),
                  _const_spec((MLA_HEADS, KV_LORA, V_DIM)), _const_spec((1, MLA_INNER))],
        out_specs=pl.BlockSpec((None, bq, MLA_INNER), lambda b, i: (b, i, 0)),
        out_shape=jax.ShapeDtypeStruct((nb, seq, MLA_INNER), BF16),
        scratch_shapes=[pltpu.VMEM((rows, 1), F32), pltpu.VMEM((rows, 1), F32), pltpu.VMEM((rows, KV_LORA), F32),
                        pltpu.VMEM((rows, bq), F32)],
        compiler_params=_params(("arbitrary", "arbitrary")),
        name="attn_prompt",
    )(qlat, qpe, ckvb, kpeb, ckvb_meta, kpeb_meta, w_uv_r, g_attn)


HIST_ROWS = 8
GROUP_W = SSD_INNER // SSD_GROUPS


def _silu(x):
    return x * jax.nn.sigmoid(x)


def _softplus(x):
    return jnp.maximum(x, 0.0) + jnp.log1p(jnp.exp(-jnp.abs(x)))


def _split3_dot(m_bf16, a):
    hi = a.astype(BF16)
    r1 = a - hi.astype(F32)
    mid = r1.astype(BF16)
    lo = (r1 - mid.astype(F32)).astype(BF16)
    dot = lambda t: jnp.dot(m_bf16, t, preferred_element_type=F32)
    return dot(hi) + dot(mid) + dot(lo)


def _conv_silu(xpad_ref, xbc, hist_rows_ref, cw_ref, cb_ref, first, lc, valid):
    h0 = HIST_ROWS - (CONV_W - 1)

    @pl.when(first)
    def _():
        xpad_ref[h0:HIST_ROWS, :] = hist_rows_ref[...]

    xpad_ref[HIST_ROWS:HIST_ROWS + lc, :] = xbc
    acc = cb_ref[...] + cw_ref[CONV_W - 1:CONV_W, :] * xbc
    for k in range(CONV_W - 1):
        acc = acc + cw_ref[k:k + 1, :] * xpad_ref[h0 + k:h0 + k + lc, :]
    tail = xpad_ref[h0 + valid:HIST_ROWS + valid, :]
    xpad_ref[h0:HIST_ROWS, :] = tail
    return _silu(acc), tail


def _gated_group_norm(y, z, g):
    yg = y * _silu(z)
    parts = []
    for gi in range(SSD_GROUPS):
        part = yg[:, gi * GROUP_W:(gi + 1) * GROUP_W]
        parts.append(part * lax.rsqrt(jnp.mean(part * part, axis=-1, keepdims=True) + EPS))
    return jnp.concatenate(parts, axis=-1) * g


def _ssd_chunk_kernel(z_ref, xbc_ref, dt16_ref, dtx_ref, hist_ref, s0_ref, cw_ref, cb_ref,
                      dtb16_ref, alog16_ref, dtbx_ref, alogx_ref, dskx_ref, g_ref,
                      y_ref, sout_ref, cout_ref, st_ref, xpad_ref, *, lc, valid):
    c = pl.program_id(1)

    @pl.when(c == 0)
    def _():
        st_ref[...] = s0_ref[...].T

    xc, tail = _conv_silu(xpad_ref, xbc_ref[...], hist_ref, cw_ref, cb_ref, c == 0, lc, valid)
    xs = xc[:, :SSD_INNER]
    row_i = lax.broadcasted_iota(jnp.int32, (lc, lc), 0)
    col_j = lax.broadcasted_iota(jnp.int32, (lc, lc), 1)
    causal = row_i >= col_j
    tril = jnp.where(causal, 1.0, 0.0).astype(BF16)

    def dt_and_cumsum(raw_ref, bias_ref, alog_ref):
        dt = _softplus(raw_ref[...] + bias_ref[...])
        if valid < lc:
            dt = jnp.where(lax.broadcasted_iota(jnp.int32, dt.shape, 0) < valid, dt, 0.0)
        return dt, _split3_dot(tril, dt * (-jnp.exp(alog_ref[...])))

    _, acs16 = dt_and_cumsum(dt16_ref, dtb16_ref, alog16_ref)
    dtx, acsx = dt_and_cumsum(dtx_ref, dtbx_ref, alogx_ref)
    acs16_t = acs16.T
    xdt = (xs * dtx).astype(BF16)
    lane = lax.broadcasted_iota(jnp.int32, (lc, 2 * SSD_HEAD_DIM), 1)
    low_half = lane < SSD_HEAD_DIM
    zero = jnp.zeros((lc, 2 * SSD_HEAD_DIM), BF16)
    st = st_ref[...]
    decay_out = jnp.exp(acsx[lc - 1:lc, :] - acsx)
    xw = (xs * decay_out * dtx).astype(BF16)
    y_parts, st_parts = [], []
    for gi in range(SSD_GROUPS):
        b0 = SSD_INNER + gi * D_STATE
        c0 = SSD_INNER + SSD_GROUPS * D_STATE + gi * D_STATE
        bm = xc[:, b0:b0 + D_STATE].astype(BF16)
        cm = xc[:, c0:c0 + D_STATE].astype(BF16)
        cb = _nt_dot(cm, bm)
        lanes = slice(gi * GROUP_W, (gi + 1) * GROUP_W)
        y_off = jnp.dot(cm, st[:, lanes].astype(BF16), preferred_element_type=F32) * jnp.exp(acsx[:, lanes])
        for pair in range(SSD_HPG // 2):
            h0 = gi * SSD_HPG + 2 * pair
            pl_lanes = slice(h0 * SSD_HEAD_DIM, (h0 + 2) * SSD_HEAD_DIM)
            xpair = xdt[:, pl_lanes]
            y_pair = None
            for t in range(2):
                h = h0 + t
                seg = acs16[:, h:h + 1] - acs16_t[h:h + 1, :]
                m = (cb * jnp.exp(jnp.where(causal, seg, -jnp.inf))).astype(BF16)
                x_h = jnp.where(low_half, xpair, zero) if t == 0 else jnp.where(low_half, zero, xpair)
                y_h = jnp.dot(m, x_h, preferred_element_type=F32)
                y_pair = y_h if y_pair is None else y_pair + y_h
            y_parts.append(y_pair + y_off[:, pl_lanes.start - lanes.start:pl_lanes.stop - lanes.start])
        inc = lax.dot_general(bm, xw[:, lanes], (((0,), (0,)), ((), ())), preferred_element_type=F32)
        st_parts.append(st[:, lanes] * jnp.exp(acsx[lc - 1:lc, lanes]) + inc)
    st_new = jnp.concatenate(st_parts, axis=-1)
    st_ref[...] = st_new
    y = jnp.concatenate(y_parts, axis=-1) + dskx_ref[...] * xs
    y_ref[...] = _gated_group_norm(y, z_ref[...], g_ref[...]).astype(y_ref.dtype)

    @pl.when(c == pl.num_programs(1) - 1)
    def _():
        sout_ref[...] = st_new.T
        cout_ref[...] = tail


def _ssd_chunked(z, xbc, dt16, dtx, hist, s0, wts, lc, valid, shared_init):
    nb, t, _ = z.shape
    assert t % lc == 0
    tok = lambda w: pl.BlockSpec((None, lc, w), lambda b, c: (b, c, 0))
    init_idx = (lambda b, c: (0, 0, 0)) if shared_init else (lambda b, c: (b, 0, 0))
    per_b = lambda r, w: pl.BlockSpec((None, r, w), lambda b, c: (b, 0, 0))
    vec = lambda w: _const_spec((1, w))
    sds = jax.ShapeDtypeStruct
    return pl.pallas_call(
        functools.partial(_ssd_chunk_kernel, lc=lc, valid=valid),
        grid=(nb, t // lc),
        in_specs=[tok(SSD_INNER), tok(CONV_DIM), tok(LANES), tok(SSD_INNER),
                  pl.BlockSpec((None, CONV_W - 1, CONV_DIM), init_idx),
                  pl.BlockSpec((None, SSD_INNER, D_STATE), init_idx),
                  _const_spec((CONV_W, CONV_DIM)), vec(CONV_DIM), vec(LANES), vec(LANES),
                  vec(SSD_INNER), vec(SSD_INNER), vec(SSD_INNER), vec(SSD_INNER)],
        out_specs=[tok(SSD_INNER), per_b(SSD_INNER, D_STATE), per_b(CONV_W - 1, CONV_DIM)],
        out_shape=[sds((nb, t, SSD_INNER), BF16), sds((nb, SSD_INNER, D_STATE), F32),
                   sds((nb, CONV_W - 1, CONV_DIM), F32)],
        scratch_shapes=[pltpu.VMEM((D_STATE, SSD_INNER), F32), pltpu.VMEM((HIST_ROWS + lc, CONV_DIM), F32)],
        compiler_params=_params(("arbitrary", "arbitrary")),
        name="ssd_chunked",
    )(z, xbc, dt16, dtx, hist, s0, wts["conv_w"], wts["conv_b"], wts["dtb16"], wts["alog16"],
      wts["dtbx"], wts["alogx"], wts["dskx"], wts["g_ssd"])


def _ssd_step_kernel(z_ref, xbc_ref, dtx_ref, hist_ref, s0_ref, cw_ref, cb_ref, dtbx_ref, alogx_ref, dskx_ref, g_ref,
                     y_ref, sout_ref, cout_ref, *, nseq, lc):
    hist_n = CONV_W - 1
    t_idx = lax.broadcasted_iota(jnp.int32, (lc, SSD_INNER), 0)
    t_idx_g = lax.broadcasted_iota(jnp.int32, (lc, GROUP_W), 0)
    a_neg = -jnp.exp(alogx_ref[...])
    for b in range(nseq):
        xbc = xbc_ref[b]
        xp = jnp.concatenate([hist_ref[b], xbc], axis=0)
        acc = cb_ref[...] + cw_ref[hist_n:CONV_W, :] * xbc
        for k in range(hist_n):
            acc = acc + cw_ref[k:k + 1, :] * xp[k:k + lc]
        xc = _silu(acc)
        cout_ref[b] = xp[lc:lc + hist_n]
        xs = xc[:, :SSD_INNER]
        dt = _softplus(dtx_ref[b] + dtbx_ref[...])
        a = dt * a_neg
        acs = jnp.zeros_like(a)
        for l in range(lc):
            acs = acs + jnp.where(t_idx >= l, a[l:l + 1, :], 0.0)
        xdt = xs * dt
        st = s0_ref[b].T
        decay_last = jnp.exp(acs[lc - 1:lc, :])
        xw = (xs * jnp.exp(acs[lc - 1:lc, :] - acs) * dt).astype(BF16)
        y_parts, st_parts = [], []
        for gi in range(SSD_GROUPS):
            b0 = SSD_INNER + gi * D_STATE
            c0 = SSD_INNER + SSD_GROUPS * D_STATE + gi * D_STATE
            bm = xc[:, b0:b0 + D_STATE]
            cm = xc[:, c0:c0 + D_STATE]
            lanes = slice(gi * GROUP_W, (gi + 1) * GROUP_W)
            acs_g = acs[:, lanes]
            y_g = jnp.dot(cm.astype(BF16), st[:, lanes].astype(BF16), preferred_element_type=F32) * jnp.exp(acs_g)
            for j in range(lc):
                cb_j = jnp.sum(cm * bm[j:j + 1, :], axis=-1, keepdims=True)
                decay = jnp.exp(jnp.where(t_idx_g >= j, acs_g - acs_g[j:j + 1, :], -jnp.inf))
                y_g = y_g + cb_j * decay * xdt[j:j + 1, lanes]
            y_parts.append(y_g)
            inc = lax.dot_general(bm.astype(BF16), xw[:, lanes], (((0,), (0,)), ((), ())),
                                  preferred_element_type=F32)
            st_parts.append(st[:, lanes] * decay_last[:, lanes] + inc)
        sout_ref[b] = jnp.concatenate(st_parts, axis=-1).T
        y = jnp.concatenate(y_parts, axis=-1) + dskx_ref[...] * xs
        y_ref[b] = _gated_group_norm(y, z_ref[b], g_ref[...]).astype(y_ref.dtype)


def _ssd_step(z, xbc, dtx, hist, s0, wts, nseq):
    nb, lc, _ = z.shape
    assert nb % nseq == 0 and lc >= CONV_W - 1
    blk = lambda r, w: pl.BlockSpec((nseq, r, w), lambda i: (i, 0, 0))
    vec = lambda w: _const_spec((1, w))
    sds = jax.ShapeDtypeStruct
    return pl.pallas_call(
        functools.partial(_ssd_step_kernel, nseq=nseq, lc=lc),
        grid=(nb // nseq,),
        in_specs=[blk(lc, SSD_INNER), blk(lc, CONV_DIM), blk(lc, SSD_INNER), blk(CONV_W - 1, CONV_DIM),
                  blk(SSD_INNER, D_STATE), _const_spec((CONV_W, CONV_DIM)), vec(CONV_DIM),
                  vec(SSD_INNER), vec(SSD_INNER), vec(SSD_INNER), vec(SSD_INNER)],
        out_specs=[blk(lc, SSD_INNER), blk(SSD_INNER, D_STATE), blk(CONV_W - 1, CONV_DIM)],
        out_shape=[sds((nb, lc, SSD_INNER), F32), sds((nb, SSD_INNER, D_STATE), F32),
                   sds((nb, CONV_W - 1, CONV_DIM), F32)],
        compiler_params=_params(("arbitrary",)),
        name="ssd_step",
    )(z, xbc, dtx, hist, s0, wts["conv_w"], wts["conv_b"], wts["dtbx"], wts["alogx"], wts["dskx"], wts["g_ssd"])


def _prep_ssd_weights(conv_w, conv_b, dt_bias, a_log, d_skip, g_ssd_norm):
    pad16 = lambda v: jnp.pad(v.reshape(1, SSD_HEADS), ((0, 0), (0, LANES - SSD_HEADS)))
    expand = lambda v: jnp.repeat(v.reshape(1, SSD_HEADS), SSD_HEAD_DIM, axis=1)
    return {"conv_w": conv_w, "conv_b": conv_b.reshape(1, CONV_DIM), "dtb16": pad16(dt_bias), "alog16": pad16(a_log),
            "dtbx": expand(dt_bias), "alogx": expand(a_log), "dskx": expand(d_skip),
            "g_ssd": g_ssd_norm.reshape(1, SSD_INNER)}


PAGES_PER_STEP = 32


def _attn_sample_kernel(pt_ref, qlat_ref, qpe_ref, cnew_ref, knew_ref, wuv_ref, g_ref, ckv_hbm, kpe_hbm,
                        o_ref, m_ref, l_ref, acc_ref, cbuf, kbuf, sems, *, lq, page):
    b = pl.program_id(0)
    j = pl.program_id(1)
    nj = pl.num_programs(1)
    step = b * nj + j
    n_steps = pl.num_programs(0) * nj
    q_lat = qlat_ref[...]
    q_pe = qpe_ref[...]

    def page_copies(stp, slot, fn):
        sb = stp // nj
        sj = stp - sb * nj
        for p in range(PAGES_PER_STEP):
            pg = pt_ref[sb, sj * PAGES_PER_STEP + p]
            rows = pl.ds(p * page, page)
            fn(pltpu.make_async_copy(ckv_hbm.at[pg], cbuf.at[slot, rows], sems.at[slot]), p)
            fn(pltpu.make_async_copy(kpe_hbm.at[pg], kbuf.at[slot, :, rows], sems.at[slot]), p)

    start = lambda c, p: c.start(priority=p % 2)

    @pl.when(step == 0)
    def _():
        page_copies(0, 0, start)

    @pl.when(step + 1 < n_steps)
    def _():
        page_copies(step + 1, (step + 1) % 2, start)

    @pl.when(j == 0)
    def _():
        m_ref[...] = jnp.full(m_ref.shape, -jnp.inf, F32)
        l_ref[...] = jnp.zeros(l_ref.shape, F32)
        acc_ref[...] = jnp.zeros(acc_ref.shape, F32)

    def absorb(s, c):
        m_old = m_ref[...]
        m_new = jnp.maximum(m_old, jnp.max(s, axis=-1, keepdims=True))
        corr = jnp.exp(m_old - m_new)
        p = jnp.exp(s - m_new)
        m_ref[...] = m_new
        l_ref[...] = l_ref[...] * corr + jnp.sum(p, axis=-1, keepdims=True)
        acc_ref[...] = acc_ref[...] * corr + jnp.dot(p.astype(BF16), c, preferred_element_type=F32)

    slot = step % 2
    page_copies(step, slot, lambda c, p: c.wait())
    c = cbuf[slot].astype(BF16)
    kp_t = kbuf[slot].astype(BF16)
    absorb((_nt_dot(q_lat, c) + jnp.dot(q_pe, kp_t, preferred_element_type=F32)) * SM_SCALE, c)

    @pl.when(j == nj - 1)
    def _():
        rows = MLA_HEADS * lq
        c_new = cnew_ref[...].astype(BF16)
        s = (_nt_dot(q_lat, c_new) + _nt_dot(q_pe, knew_ref[...].astype(BF16))) * SM_SCALE
        q_idx = lax.broadcasted_iota(jnp.int32, (rows, lq), 0) % lq
        k_idx = lax.broadcasted_iota(jnp.int32, (rows, lq), 1)
        absorb(jnp.where(q_idx >= k_idx, s, -jnp.inf), c_new)
        o_ref[...] = _attn_out_proj(acc_ref[...] / l_ref[...], wuv_ref, g_ref, lq).astype(o_ref.dtype)


def _attn_sample(page_table, qlat, qpe, c_new, k_new, pool_ckv, pool_kpe, w_uv_r, g_attn):
    nb, rows, _ = qlat.shape
    lq = rows // MLA_HEADS
    n_pages = page_table.shape[1]
    page = pool_ckv.shape[1]
    assert n_pages % PAGES_PER_STEP == 0
    per_b = lambda r, w: pl.BlockSpec((None, r, w), lambda b, j, pt: (b, 0, 0))
    const = lambda shape: pl.BlockSpec(shape, lambda b, j, pt: (0,) * len(shape), pipeline_mode=pl.Buffered(1))
    keys = PAGES_PER_STEP * page
    return pl.pallas_call(
        functools.partial(_attn_sample_kernel, lq=lq, page=page),
        grid_spec=pltpu.PrefetchScalarGridSpec(
            num_scalar_prefetch=1,
            grid=(nb, n_pages // PAGES_PER_STEP),
            in_specs=[per_b(rows, KV_LORA), per_b(rows, QK_ROPE), per_b(lq, KV_LORA), per_b(lq, QK_ROPE),
                      const((MLA_HEADS, KV_LORA, V_DIM)), const((1, MLA_INNER)),
                      pl.BlockSpec(memory_space=pl.ANY), pl.BlockSpec(memory_space=pl.ANY)],
            out_specs=pl.BlockSpec((None, lq, MLA_INNER), lambda b, j, pt: (b, 0, 0)),
            scratch_shapes=[pltpu.VMEM((rows, 1), F32), pltpu.VMEM((rows, 1), F32), pltpu.VMEM((rows, KV_LORA), F32),
                            pltpu.VMEM((2, keys, KV_LORA), F32), pltpu.VMEM((2, QK_ROPE, keys), F32),
                            pltpu.SemaphoreType.DMA((2,))]),
        out_shape=jax.ShapeDtypeStruct((nb, lq, MLA_INNER), F32),
        compiler_params=_params(("arbitrary", "arbitrary")),
        name="attn_sample",
    )(page_table, qlat, qpe, c_new, k_new, w_uv_r, g_attn, pool_ckv, pool_kpe)


def _merge_router_kernel(ha_ref, ya_ref, oa_ref, hb_ref, yb_ref, ob_ref, wa_ref, wb_ref, gffn_ref, wr_ref, br_ref,
                         h2_ref, xn_ref, topi_ref, gate_ref, rank_ref, cnt_ref, carry_ref, *, tm, steps_a):
    i = pl.program_id(0)

    @pl.when(i == 0)
    def _():
        carry_ref[...] = jnp.zeros(carry_ref.shape, F32)

    first = i < steps_a
    pick = lambda a_ref, b_ref, dt: jnp.where(first, a_ref[...].astype(dt), b_ref[...].astype(dt))
    mix = (jnp.dot(pick(ya_ref, yb_ref, BF16), wa_ref[...], preferred_element_type=F32)
           + jnp.dot(pick(oa_ref, ob_ref, BF16), wb_ref[...], preferred_element_type=F32))
    h2 = pick(ha_ref, hb_ref, F32) + mix
    h2_ref[...] = h2
    xn32 = _rms(h2, gffn_ref[...])
    for ct in range(ROW_TILES):
        xn_ref[pl.ds(ct, tm, stride=ROW_TILES), :] = xn32[:, ct * LANES:(ct + 1) * LANES]
    xn = xn32.astype(BF16)
    lane = lax.broadcasted_iota(jnp.int32, (tm, LANES), 1)
    logits = jnp.dot(xn, wr_ref[...], preferred_element_type=F32) + br_ref[...]
    logits = jnp.where(lane < N_EXPERTS, logits, -jnp.inf)
    vals, idxs = [], []
    for _ in range(TOP_K):
        v = jnp.max(logits, axis=-1, keepdims=True)
        idx = jnp.min(jnp.where(logits == v, lane, LANES), axis=-1, keepdims=True)
        vals.append(v)
        idxs.append(idx)
        logits = jnp.where(lane == idx, -jnp.inf, logits)
    exps = [jnp.exp(v - vals[0]) for v in vals]
    denom = exps[0]
    for e in exps[1:]:
        denom = denom + e
    hit = [lane == idx for idx in idxs]
    onehot = jnp.zeros((tm, LANES), F32)
    for hk in hit:
        onehot = onehot + jnp.where(hk, 1.0, 0.0)
    r_i = lax.broadcasted_iota(jnp.int32, (tm, tm), 0)
    c_j = lax.broadcasted_iota(jnp.int32, (tm, tm), 1)
    strict_tril = jnp.where(r_i > c_j, 1.0, 0.0).astype(BF16)
    before = jnp.dot(strict_tril, onehot.astype(BF16), preferred_element_type=F32) + carry_ref[...]
    carry_ref[...] = carry_ref[...] + jnp.sum(onehot, axis=0, keepdims=True)
    topi = jnp.zeros((tm, LANES), jnp.int32)
    gate = jnp.zeros((tm, LANES), F32)
    rank = jnp.zeros((tm, LANES), jnp.int32)
    for k in range(TOP_K):
        rk = jnp.sum(jnp.where(hit[k], before, 0.0), axis=-1, keepdims=True).astype(jnp.int32)
        topi = jnp.where(lane == k, idxs[k], topi)
        gate = jnp.where(lane == k, exps[k] / denom, gate)
        rank = jnp.where(lane == k, rk, rank)
    topi_ref[...] = topi
    gate_ref[...] = gate
    rank_ref[...] = rank
    cnt_ref[...] = carry_ref[...]


def _merge_router(group_a, group_b, wts, tm):
    rows_a, rows_b = group_a[0].shape[0], group_b[0].shape[0]
    assert rows_a % tm == 0 and rows_b % tm == 0
    steps_a, steps_b = rows_a // tm, rows_b // tm
    rows = rows_a + rows_b
    spec_a = lambda w: pl.BlockSpec((tm, w), lambda i: (jnp.minimum(i, steps_a - 1), 0))
    spec_b = lambda w: pl.BlockSpec((tm, w), lambda i: (jnp.maximum(i - steps_a, 0), 0))
    row = lambda w: pl.BlockSpec((tm, w), lambda i: (i, 0))
    widths = (D_MODEL, SSD_INNER, MLA_INNER)
    sds = jax.ShapeDtypeStruct
    return pl.pallas_call(
        functools.partial(_merge_router_kernel, tm=tm, steps_a=steps_a),
        grid=(steps_a + steps_b,),
        in_specs=[spec_a(w) for w in widths] + [spec_b(w) for w in widths]
                 + [_const_spec((SSD_INNER, D_MODEL)), _const_spec((MLA_INNER, D_MODEL)), _const_spec((1, D_MODEL)),
                    _const_spec((D_MODEL, LANES)), _const_spec((1, LANES))],
        out_specs=[row(D_MODEL), pl.BlockSpec((tm * ROW_TILES, LANES), lambda i: (i, 0)), row(LANES), row(LANES),
                   row(LANES),
                   pl.BlockSpec((1, LANES), lambda i: (0, 0))],
        out_shape=[sds((rows, D_MODEL), F32), sds((rows * ROW_TILES, LANES), F32), sds((rows, LANES), jnp.int32),
                   sds((rows, LANES), F32), sds((rows, LANES), jnp.int32), sds((1, LANES), F32)],
        scratch_shapes=[pltpu.VMEM((1, LANES), F32)],
        compiler_params=_params(("arbitrary",)),
        name="merge_router",
    )(*group_a, *group_b, wts["w_out_a"], wts["w_out_b"], wts["g_ffn"], wts["w_router"], wts["b_router"])


MOE_TM = 1152
MOE_TS = 384
MOE_TF = 256
MOE_GATHER_ROWS = MOE_TM // (D_FF // MOE_TF)
COMBINE_TM = 256


def _moe_kernel(te_ref, nv_ref, src_ref, xn_hbm, wg_ref, wu_ref, wd_ref, bg_ref, bu_ref, bd_ref, y_ref,
                xland, xbf, wg_s, wu_s, wd_s, sem):
    del te_ref
    i = pl.program_id(0)
    f = pl.program_id(1)
    n_tiles = pl.num_programs(0)
    last_f = pl.num_programs(1) - 1
    nv = nv_ref[i]
    n_sub_max = MOE_TM // MOE_TS
    sub_tiles = MOE_TS // SUBLANES
    y4 = _tile_view(y_ref)
    land4 = _tile_view(xland)

    def row_copy(src_row, dst_row):
        src = xn_hbm.at[pl.ds(src_row * ROW_TILES, ROW_TILES)]
        if isinstance(dst_row, int):
            d_rt, d_sl = dst_row // SUBLANES, dst_row % SUBLANES
        else:
            d_rt, d_sl = lax.shift_right_logical(dst_row, 3), lax.bitwise_and(dst_row, SUBLANES - 1)
        return pltpu.make_async_copy(src, xland.at[d_rt, :, d_sl], sem)

    def wait_tile_rows():
        for r in range(MOE_TM):
            row_copy(0, r).wait()

    @pl.when((i == 0) & (f == 0))
    def _():
        lax.fori_loop(0, MOE_TM, lambda r, c: (row_copy(src_ref[r], r).start(), c)[1], 0)

    requested = (i == 0) | (nv_ref[jnp.maximum(i - 1, 0)] > 0)

    @pl.when((f == 0) & requested)
    def _():
        wait_tile_rows()
        for s in range(n_sub_max):
            @pl.when(s * MOE_TS < nv)
            def _():
                for ct in range(ROW_TILES):
                    piece = land4[s * sub_tiles:(s + 1) * sub_tiles, ct].reshape(MOE_TS, LANES)
                    xbf[s * MOE_TS:(s + 1) * MOE_TS, ct * LANES:(ct + 1) * LANES] = piece.astype(BF16)

    for s in range(n_sub_max):
        tiles = slice(s * sub_tiles, (s + 1) * sub_tiles)

        @pl.when((f == 0) & (s * MOE_TS < nv))
        def _():
            for ct in range(ROW_TILES):
                y4[tiles, ct] = jnp.broadcast_to(bd_ref[:, ct * LANES:(ct + 1) * LANES], (sub_tiles, SUBLANES, LANES))

        @pl.when((f == 0) & (s * MOE_TS >= nv))
        def _():
            y4[tiles] = jnp.zeros((sub_tiles, ROW_TILES, SUBLANES, LANES), F32)

    def tile_step(n_sub):
        wg_s[...] = wg_ref[...].astype(BF16)
        wu_s[...] = wu_ref[...].astype(BF16)
        wd_s[...] = wd_ref[...].astype(BF16)
        for r in range(MOE_GATHER_ROWS):
            row = f * MOE_GATHER_ROWS + r
            row_copy(src_ref[(i + 1) * MOE_TM + row], row).start(priority=1)
        for s in range(n_sub):
            x = xbf[s * MOE_TS:(s + 1) * MOE_TS, :]
            g = jnp.minimum(jnp.dot(x, wg_s[...], preferred_element_type=F32) + bg_ref[...], SWIGLU_LIMIT)
            u = jnp.clip(jnp.dot(x, wu_s[...], preferred_element_type=F32) + bu_ref[...],
                         -SWIGLU_LIMIT, SWIGLU_LIMIT)
            hdn = ((u + 1.0) * (g * jax.nn.sigmoid(SWIGLU_ALPHA * g))).astype(BF16)
            part = jnp.dot(hdn, wd_s[...], preferred_element_type=F32)
            tiles = slice(s * sub_tiles, (s + 1) * sub_tiles)
            for ct in range(ROW_TILES):
                y4[tiles, ct] = y4[tiles, ct] + part[:, ct * LANES:(ct + 1) * LANES].reshape(
                    sub_tiles, SUBLANES, LANES)

    n_sub_valid = (nv + MOE_TS - 1) // MOE_TS
    for k in range(1, n_sub_max + 1):
        pl.when(n_sub_valid == k)(functools.partial(tile_step, k))

    @pl.when((i == n_tiles - 1) & (f == last_f) & (nv > 0))
    def _():
        wait_tile_rows()


def _moe_experts(tile_expert, tile_nvalid, src_rows, xn_rows, w_gate, b_gate, w_up, b_up, w_down, b_down):
    n_tiles = src_rows.shape[0] // MOE_TM - 1
    n_f = D_FF // MOE_TF
    assert MOE_GATHER_ROWS * n_f == MOE_TM
    fidx = lambda i, f, nv: jnp.where(nv[i] > 0, f, n_f - 1)
    wspec = lambda shape, imap: pl.BlockSpec(shape, lambda i, f, te, nv, src: imap(te[i], fidx(i, f, nv)))
    return pl.pallas_call(
        _moe_kernel,
        grid_spec=pltpu.PrefetchScalarGridSpec(
            num_scalar_prefetch=3, grid=(n_tiles, n_f),
            in_specs=[pl.BlockSpec(memory_space=pl.ANY),
                      wspec((None, D_MODEL, MOE_TF), lambda e, f: (e, 0, f)),
                      wspec((None, D_MODEL, MOE_TF), lambda e, f: (e, 0, f)),
                      wspec((None, MOE_TF, D_MODEL), lambda e, f: (e, f, 0)),
                      wspec((None, 1, MOE_TF), lambda e, f: (e, 0, f)),
                      wspec((None, 1, MOE_TF), lambda e, f: (e, 0, f)),
                      wspec((None, 1, D_MODEL), lambda e, f: (e, 0, 0))],
            out_specs=_gatherable_rows_spec(MOE_TM, lambda i, f, te, nv, src: i),
            scratch_shapes=[_row_landing_scratch(MOE_TM), pltpu.VMEM((MOE_TM, D_MODEL), BF16),
                            pltpu.VMEM((D_MODEL, MOE_TF), BF16), pltpu.VMEM((D_MODEL, MOE_TF), BF16),
                            pltpu.VMEM((MOE_TF, D_MODEL), BF16), pltpu.SemaphoreType.DMA(())]),
        out_shape=_gatherable_rows_shape(n_tiles * MOE_TM),
        compiler_params=_params(("arbitrary", "arbitrary")),
        name="moe_experts",
    )(tile_expert, tile_nvalid, src_rows, xn_rows.reshape(-1, 1, LANES), w_gate, w_up, w_down,
      b_gate.reshape(N_EXPERTS, 1, D_FF), b_up.reshape(N_EXPERTS, 1, D_FF), b_down.reshape(N_EXPERTS, 1, D_MODEL))


def _combine_kernel(pos_ref, h2_ref, gate_ref, y_hbm, gfin_ref, out_ref, *scratch, tm, row_offset):
    bufs, sems = scratch[:2 * TOP_K], scratch[2 * TOP_K]
    i = pl.program_id(0)
    n = pl.num_programs(0)

    def row_copy(src_row, slot, k, t):
        return _gather_row_copy(y_hbm, src_row, bufs[slot * TOP_K + k], t, sems.at[slot])

    def start_copies(step, slot):
        def body(t, carry):
            for k in range(TOP_K):
                row_copy(pos_ref[(row_offset + step * tm + t) * TOP_K + k], slot, k, t).start(priority=k % 2)
            return carry
        lax.fori_loop(0, tm, body, 0, unroll=2)

    def wait_copies(slot):
        for t in range(tm):
            for k in range(TOP_K):
                row_copy(0, slot, k, t).wait()

    def for_slot(slot_value, fn):
        for slot in range(2):
            pl.when(slot_value == slot)(functools.partial(fn, slot))

    @pl.when(i == 0)
    def _():
        start_copies(0, 0)

    @pl.when(i + 1 < n)
    def _():
        for_slot((i + 1) % 2, lambda slot: start_copies(i + 1, slot))

    def finish(slot):
        wait_copies(slot)
        gate = gate_ref[...]
        gates = [jnp.broadcast_to(gate[:, k:k + 1], (tm, LANES)) for k in range(TOP_K)]
        views = [_tile_view(bufs[slot * TOP_K + k]) for k in range(TOP_K)]
        parts = []
        for ct in range(ROW_TILES):
            acc = h2_ref[:, ct * LANES:(ct + 1) * LANES]
            for k in range(TOP_K):
                acc = acc + gates[k] * views[k][:, ct].reshape(tm, LANES)
            parts.append(acc)
        out_ref[...] = _rms(jnp.concatenate(parts, axis=-1), gfin_ref[...])

    for_slot(i % 2, finish)


def _combine(pos_flat, h2, gate, y_rows, g_final, row_offset, rows):
    tm = COMBINE_TM
    assert rows % tm == 0 and row_offset % tm == 0
    off = row_offset // tm
    return pl.pallas_call(
        functools.partial(_combine_kernel, tm=tm, row_offset=row_offset),
        grid_spec=pltpu.PrefetchScalarGridSpec(
            num_scalar_prefetch=1, grid=(rows // tm,),
            in_specs=[pl.BlockSpec((tm, D_MODEL), lambda i, pos: (i + off, 0)),
                      pl.BlockSpec((tm, LANES), lambda i, pos: (i + off, 0)),
                      pl.BlockSpec(memory_space=pl.ANY),
                      pl.BlockSpec((1, D_MODEL), lambda i, pos: (0, 0))],
            out_specs=pl.BlockSpec((tm, D_MODEL), lambda i, pos: (i, 0)),
            scratch_shapes=[_row_landing_scratch(tm) for _ in range(2 * TOP_K)] + [pltpu.SemaphoreType.DMA((2,))]),
        out_shape=jax.ShapeDtypeStruct((rows, D_MODEL), F32),
        compiler_params=_params(("arbitrary",)),
        name="moe_combine",
    )(pos_flat, h2, gate, y_rows, g_final)


def _routing_tables(topi, rank, counts, n_rows):
    counts = counts[0, :N_EXPERTS].astype(jnp.int32)
    tiles_per = (counts + MOE_TM - 1) // MOE_TM
    tile_end = jnp.cumsum(tiles_per)
    tile_start = tile_end - tiles_per
    n_tiles_max = (n_rows * TOP_K + N_EXPERTS * (MOE_TM - 1)) // MOE_TM
    pos = tile_start[topi] * MOE_TM + rank
    src = jnp.zeros(((n_tiles_max + 1) * MOE_TM,), jnp.int32).at[pos.reshape(-1)].set(
        jnp.repeat(jnp.arange(n_rows, dtype=jnp.int32), TOP_K))
    tile_id = jnp.arange(n_tiles_max, dtype=jnp.int32)
    n_used = tile_end[-1]
    last = jnp.maximum(n_used - 1, 0)
    tile_row = jnp.minimum(tile_id, last)
    owner = jnp.sum((tile_end[None, :] <= tile_row[:, None]).astype(jnp.int32), axis=1)
    tile_expert = jnp.minimum(owner, N_EXPERTS - 1)
    in_tile = tile_row - tile_start[tile_expert]
    nvalid = jnp.clip(counts[tile_expert] - in_tile * MOE_TM, 0, MOE_TM)
    nvalid = jnp.where(tile_id < n_used, nvalid, 0).astype(jnp.int32)
    return pos.reshape(-1).astype(jnp.int32), src, tile_expert, nvalid


def kernel(x_prompt, x_sample, cache_ckv, cache_kpe, state_ssm, state_conv, page_table, meta_tokens, g_mix_norm, w_in, g_q_norm, w_q_up, g_kv_norm, w_uk, w_uv, conv_w, conv_b, dt_bias, a_log, d_skip, g_ssd_norm, g_attn_out, w_out, g_ffn_norm, w_router, b_router, w_gate, b_gate, w_up, b_up, w_down, b_down, g_final_norm):
    assert w_in.shape[0] == 1, "single-layer model"
    bp, seq = x_prompt.shape[:2]
    bs, dseq = x_sample.shape[:2]
    rows_p, rows_s = bp * seq, bs * dseq
    past_len = page_table.shape[1] * cache_ckv.shape[2]
    wm = _prep_mixer_weights(g_mix_norm[0], w_in[0], g_q_norm[0], w_q_up[0], g_kv_norm[0], w_uk[0])
    ws = _prep_ssd_weights(conv_w[0], conv_b[0], dt_bias[0], a_log[0], d_skip[0], g_ssd_norm[0])
    w_uv_r = jnp.transpose(w_uv[0], (1, 0, 2)).astype(BF16)
    g_attn = g_attn_out[0].reshape(1, MLA_INNER)
    wr = {"w_out_a": w_out[0][:SSD_INNER].astype(BF16), "w_out_b": w_out[0][SSD_INNER:].astype(BF16),
          "g_ffn": g_ffn_norm[0].reshape(1, D_MODEL),
          "w_router": jnp.pad(w_router[0], ((0, 0), (0, LANES - N_EXPERTS))).astype(BF16),
          "b_router": jnp.pad(b_router[0], (0, LANES - N_EXPERTS)).reshape(1, LANES)}

    tm = 256
    a_m = _inproj(meta_tokens, *_rope_tables(jnp.arange(N_META)), wm, N_META)
    a_p = _inproj(x_prompt.reshape(rows_p, D_MODEL), *_rope_tables(N_META + jnp.arange(seq)), wm, tm)
    cs_s = [jnp.tile(t, (tm // dseq, 1)) for t in _rope_tables(past_len + jnp.arange(dseq))]
    a_s = _inproj(x_sample.reshape(rows_s, D_MODEL), *cs_s, wm, tm)
    qlat_m, qpe_m, ckv_m, ckvb_m, kpe_m, kpeb_m, z_m, xbc_m, dt16_m, dtx_m = a_m
    qlat_p, qpe_p, ckv_p, ckvb_p, kpe_p, kpeb_p, z_p, xbc_p, dt16_p, dtx_p = a_p
    qlat_s, qpe_s, ckv_s, ckvb_s, kpe_s, kpeb_s, z_s, xbc_s, dt16_s, dtx_s = a_s
    del qlat_m, qpe_m, ckvb_s, kpeb_s, dt16_s

    heads_first = lambda q, b, t: q.reshape(MLA_HEADS, b, t, q.shape[-1])
    o_p = _attn_prompt(heads_first(qlat_p, bp, seq), heads_first(qpe_p, bp, seq),
                       ckvb_p.reshape(bp, seq, KV_LORA), kpeb_p.reshape(bp, seq, QK_ROPE),
                       ckvb_m, kpeb_m, w_uv_r, g_attn, 256)
    per_seq = lambda q: jnp.transpose(heads_first(q, bs, dseq), (1, 0, 2, 3)).reshape(bs, MLA_HEADS * dseq, q.shape[-1])
    o_s = _attn_sample(page_table, per_seq(qlat_s), per_seq(qpe_s), ckv_s.reshape(bs, dseq, KV_LORA),
                       kpe_s.reshape(bs, dseq, QK_ROPE), cache_ckv[0], jnp.swapaxes(cache_kpe[0], 1, 2),
                       w_uv_r, g_attn)

    pad_meta = lambda v: jnp.pad(v, ((0, CHUNK - N_META), (0, 0)))[None]
    _, ssm_m, conv_m = _ssd_chunked(pad_meta(z_m), pad_meta(xbc_m), pad_meta(dt16_m), pad_meta(dtx_m),
                                    jnp.zeros((1, CONV_W - 1, CONV_DIM), F32),
                                    jnp.zeros((1, SSD_INNER, D_STATE), F32), ws, CHUNK, N_META, True)
    per_b = lambda v, b, t: v.reshape(b, t, v.shape[-1])
    y_p, ssm_p, conv_p = _ssd_chunked(per_b(z_p, bp, seq), per_b(xbc_p, bp, seq), per_b(dt16_p, bp, seq),
                                      per_b(dtx_p, bp, seq), conv_m, ssm_m, ws, CHUNK, CHUNK, True)
    y_s, ssm_s, conv_s = _ssd_step(per_b(z_s, bs, dseq), per_b(xbc_s, bs, dseq), per_b(dtx_s, bs, dseq),
                                   state_conv[0], state_ssm[0].reshape(bs, SSD_INNER, D_STATE), ws, 8)

    total = rows_p + rows_s
    h2, xn_rows, topi, gate, rank, counts = _merge_router(
        (x_prompt.reshape(rows_p, D_MODEL), y_p.reshape(rows_p, SSD_INNER), o_p.reshape(rows_p, MLA_INNER)),
        (x_sample.reshape(rows_s, D_MODEL), y_s.reshape(rows_s, SSD_INNER), o_s.reshape(rows_s, MLA_INNER)),
        wr, tm)

    pos, src, tile_expert, tile_nvalid = _routing_tables(topi[:, :TOP_K], rank[:, :TOP_K], counts, total)
    y_rows = _moe_experts(tile_expert, tile_nvalid, src, xn_rows, w_gate[0], b_gate[0], w_up[0],
                          b_up[0], w_down[0], b_down[0])
    g_fin = g_final_norm.reshape(1, D_MODEL)
    y_prompt = _combine(pos, h2, gate, y_rows, g_fin, 0, rows_p).reshape(bp, seq, D_MODEL)
    y_sample = _combine(pos, h2, gate, y_rows, g_fin, rows_p, rows_s).reshape(bs, dseq, D_MODEL)

    with_meta = lambda m, p, w: jnp.concatenate(
        [jnp.broadcast_to(m[None], (bp, N_META, w)), p.reshape(bp, seq, w)], axis=1)[None]
    return (y_prompt, y_sample,
            with_meta(ckv_m, ckv_p, KV_LORA), with_meta(kpe_m, kpe_p, QK_ROPE),
            ssm_p.reshape(1, bp, SSD_HEADS, SSD_HEAD_DIM, D_STATE), conv_p[None],
            ckv_s.reshape(1, bs, dseq, KV_LORA), kpe_s.reshape(1, bs, dseq, QK_ROPE),
            ssm_s.reshape(1, bs, SSD_HEADS, SSD_HEAD_DIM, D_STATE), conv_s[None])
```

```python
import functools
import math

import jax
import jax.numpy as jnp
from jax import lax
from jax.experimental import pallas as pl
from jax.experimental.pallas import tpu as pltpu

F32 = jnp.float32
BF16 = jnp.bfloat16

D_MODEL = 2048
N_META = 16
EPS = 1e-6
SSD_HEADS = 16
SSD_HEAD_DIM = 64
SSD_INNER = SSD_HEADS * SSD_HEAD_DIM
SSD_GROUPS = 2
SSD_HPG = SSD_HEADS // SSD_GROUPS
D_STATE = 128
CONV_W = 4
CONV_DIM = SSD_INNER + 2 * SSD_GROUPS * D_STATE
CHUNK = 128
MLA_HEADS = 8
Q_LORA = 512
KV_LORA = 512
QK_NOPE = 128
QK_ROPE = 64
V_DIM = 128
MLA_INNER = MLA_HEADS * V_DIM
ROPE_THETA = 10000.0
SM_SCALE = (QK_NOPE + QK_ROPE) ** -0.5
N_EXPERTS = 32
TOP_K = 4
D_FF = 2048
SWIGLU_LIMIT = 7.0
SWIGLU_ALPHA = 1.702

LANES = 128
SUBLANES = 8
ROW_TILES = D_MODEL // LANES
VMEM_LIMIT_BYTES = 56 * 1024 * 1024

U_QA = 0
U_KV = U_QA + Q_LORA
U_Z = U_KV + KV_LORA
U_XBC = U_Z + SSD_INNER
U_KPE = U_XBC + CONV_DIM
U_TAIL = U_KPE + 2 * QK_ROPE
U_DTX = U_TAIL + LANES
U_WIDTH = U_DTX + SSD_INNER
Q_NOPE_W = MLA_HEADS * QK_NOPE
Q_ROPE_W = MLA_HEADS * QK_ROPE
Q_UP_WIDTH = Q_NOPE_W + 2 * Q_ROPE_W


def _params(semantics):
    return pltpu.CompilerParams(dimension_semantics=semantics, vmem_limit_bytes=VMEM_LIMIT_BYTES)


def _const_spec(shape):
    nd = len(shape)
    return pl.BlockSpec(shape, lambda *_: (0,) * nd, pipeline_mode=pl.Buffered(1))


def _rms(x, g):
    return x * lax.rsqrt(jnp.mean(x * x, axis=-1, keepdims=True) + EPS) * g


def _gatherable_rows_shape(rows):
    return jax.ShapeDtypeStruct((rows // SUBLANES, ROW_TILES, SUBLANES, 1, LANES), F32)


def _gatherable_rows_spec(tm, row_block):
    return pl.BlockSpec((tm // SUBLANES, ROW_TILES, SUBLANES, 1, LANES), lambda *a: (row_block(*a), 0, 0, 0, 0))


def _tile_view(ref):
    return ref.reshape(ref.shape[0], ROW_TILES, SUBLANES, LANES)


def _row_landing_scratch(rows):
    return pltpu.VMEM((rows // SUBLANES, ROW_TILES, SUBLANES, 1, LANES), F32)


def _gather_row_copy(src, src_row, dst, dst_row, sem):
    def split(r):
        if isinstance(r, int):
            return r // SUBLANES, r % SUBLANES
        return lax.shift_right_logical(r, 3), lax.bitwise_and(r, SUBLANES - 1)

    s_rt, s_sl = split(src_row)
    d_rt, d_sl = split(dst_row)
    return pltpu.make_async_copy(src.at[s_rt, :, s_sl], dst.at[d_rt, :, d_sl], sem)


def _rot_half_cols(w):
    half = QK_ROPE // 2
    return jnp.concatenate([-w[..., half:], w[..., :half]], axis=-1)


def _inproj_kernel(x_ref, gmix_ref, win_ref, gq_ref, wq_ref, wuk_ref, gkv_ref, cs_q_ref, cs_k_ref,
                   qlat_ref, qpe_ref, ckv_ref, ckvb_ref, kpe_ref, kpeb_ref, z_ref, xbc_ref, tail_ref, dtx_ref):
    xn = _rms(x_ref[...], gmix_ref[...]).astype(BF16)
    u = jnp.dot(xn, win_ref[...], preferred_element_type=F32)
    z_ref[...] = u[:, U_Z:U_Z + SSD_INNER]
    xbc_ref[...] = u[:, U_XBC:U_XBC + CONV_DIM]
    tail_ref[...] = u[:, U_TAIL:U_TAIL + LANES]
    dtx_ref[...] = u[:, U_DTX:U_DTX + SSD_INNER]
    ckv = _rms(u[:, U_KV:U_KV + KV_LORA], gkv_ref[...])
    ckv_ref[...] = ckv
    ckvb_ref[...] = ckv.astype(BF16)
    kprod = u[:, U_KPE:U_KPE + 2 * QK_ROPE] * cs_k_ref[...]
    kpe = kprod[:, :QK_ROPE] + kprod[:, QK_ROPE:]
    kpe_ref[...] = kpe
    kpeb_ref[...] = kpe.astype(BF16)
    qn = _rms(u[:, U_QA:U_QA + Q_LORA], gq_ref[...]).astype(BF16)
    q = jnp.dot(qn, wq_ref[...], preferred_element_type=F32)
    cs_q = cs_q_ref[...]
    qpe = (q[:, Q_NOPE_W:Q_NOPE_W + Q_ROPE_W] * cs_q[:, :Q_ROPE_W]
           + q[:, Q_NOPE_W + Q_ROPE_W:] * cs_q[:, Q_ROPE_W:]).astype(BF16)
    for h in range(MLA_HEADS):
        qpe_ref[h] = qpe[:, h * QK_ROPE:(h + 1) * QK_ROPE]
        qn_h = q[:, h * QK_NOPE:(h + 1) * QK_NOPE].astype(BF16)
        qlat_ref[h] = jnp.dot(qn_h, wuk_ref[h], preferred_element_type=F32).astype(BF16)


def _inproj(x, cs_q, cs_k, wts, tm):
    rows = x.shape[0]
    pos_rows = cs_q.shape[0]
    assert rows % tm == 0 and pos_rows % tm == 0
    n_pos = pos_rows // tm
    row = lambda w: pl.BlockSpec((tm, w), lambda i: (i, 0))
    pos = lambda w: pl.BlockSpec((tm, w), lambda i: (i % n_pos, 0))
    head = lambda w: pl.BlockSpec((MLA_HEADS, tm, w), lambda i: (0, i, 0))
    sds = jax.ShapeDtypeStruct
    return pl.pallas_call(
        _inproj_kernel,
        grid=(rows // tm,),
        in_specs=[row(D_MODEL), _const_spec((1, D_MODEL)), _const_spec((D_MODEL, U_WIDTH)),
                  _const_spec((1, Q_LORA)), _const_spec((Q_LORA, Q_UP_WIDTH)),
                  _const_spec((MLA_HEADS, QK_NOPE, KV_LORA)), _const_spec((1, KV_LORA)),
                  pos(2 * Q_ROPE_W), pos(2 * QK_ROPE)],
        out_specs=[head(KV_LORA), head(QK_ROPE), row(KV_LORA), row(KV_LORA), row(QK_ROPE), row(QK_ROPE),
                   row(SSD_INNER), row(CONV_DIM), row(LANES), row(SSD_INNER)],
        out_shape=[sds((MLA_HEADS, rows, KV_LORA), BF16), sds((MLA_HEADS, rows, QK_ROPE), BF16),
                   sds((rows, KV_LORA), F32), sds((rows, KV_LORA), BF16),
                   sds((rows, QK_ROPE), F32), sds((rows, QK_ROPE), BF16),
                   sds((rows, SSD_INNER), F32), sds((rows, CONV_DIM), F32), sds((rows, LANES), F32),
                   sds((rows, SSD_INNER), F32)],
        compiler_params=_params(("arbitrary",)),
        name="inproj",
    )(x, wts["g_mix"], wts["w_in"], wts["g_q"], wts["w_q"], wts["w_uk"], wts["g_kv"], cs_q, cs_k)


def _rope_tables(pos):
    inv_freq = ROPE_THETA ** (-jnp.arange(0, QK_ROPE, 2, dtype=F32) / QK_ROPE)
    ang = pos.astype(F32)[:, None] * inv_freq[None, :]
    cos = jnp.tile(jnp.cos(ang), (1, 2))
    sin = jnp.tile(jnp.sin(ang), (1, 2))
    cs_k = jnp.concatenate([cos, sin], axis=-1)
    cs_q = jnp.concatenate([jnp.tile(cos, (1, MLA_HEADS)), jnp.tile(sin, (1, MLA_HEADS))], axis=-1)
    return cs_q, cs_k


def _prep_mixer_weights(g_mix_norm, w_in, g_q_norm, w_q_up, g_kv_norm, w_uk):
    s = (0, Q_LORA, Q_LORA + KV_LORA, Q_LORA + KV_LORA + QK_ROPE,
         Q_LORA + KV_LORA + QK_ROPE + SSD_INNER, Q_LORA + KV_LORA + QK_ROPE + SSD_INNER + CONV_DIM)
    w_qa, w_kv, w_kpe, w_z, w_xbc, w_dt = (w_in[:, s[i]:(s[i + 1] if i + 1 < len(s) else None)] for i in range(6))
    pad = jnp.zeros((D_MODEL, LANES - SSD_HEADS), w_in.dtype)
    w_in_r = jnp.concatenate([w_qa, w_kv, w_z, w_xbc, w_kpe, _rot_half_cols(w_kpe), w_dt, pad,
                              jnp.repeat(w_dt, SSD_HEAD_DIM, axis=1)], axis=1)
    wq = w_q_up.reshape(Q_LORA, MLA_HEADS, QK_NOPE + QK_ROPE)
    wq_nope = wq[:, :, :QK_NOPE].reshape(Q_LORA, Q_NOPE_W)
    wq_rope = wq[:, :, QK_NOPE:]
    w_q_r = jnp.concatenate([wq_nope, wq_rope.reshape(Q_LORA, Q_ROPE_W),
                             _rot_half_cols(wq_rope).reshape(Q_LORA, Q_ROPE_W)], axis=1)
    return {
        "g_mix": g_mix_norm.reshape(1, D_MODEL), "w_in": w_in_r.astype(BF16),
        "g_q": g_q_norm.reshape(1, Q_LORA), "w_q": w_q_r.astype(BF16),
        "w_uk": jnp.transpose(w_uk, (1, 2, 0)).astype(BF16),
        "g_kv": g_kv_norm.reshape(1, KV_LORA),
    }


def _nt_dot(a, b):
    return lax.dot_general(a, b, (((1,), (1,)), ((), ())), preferred_element_type=F32)


def _attn_out_proj(o_lat, wuv_ref, g_ref, rows):
    o_b = o_lat.astype(BF16)
    o = jnp.concatenate([jnp.dot(o_b[h * rows:(h + 1) * rows], wuv_ref[h], preferred_element_type=F32)
                         for h in range(MLA_HEADS)], axis=-1)
    return _rms(o, g_ref[...])


def _attn_prompt_kernel(qlat_ref, qpe_ref, ckv_ref, kpe_ref, ckvm_ref, kpem_ref, wuv_ref, g_ref,
                        o_ref, m_ref, l_ref, acc_ref, s_ref, *, bq):
    qi = pl.program_id(1)
    rows = MLA_HEADS * bq
    q_lat = qlat_ref[...].reshape(rows, KV_LORA)
    q_pe = qpe_ref[...].reshape(rows, QK_ROPE)

    def scores(c, kp):
        return (_nt_dot(q_lat, c) + _nt_dot(q_pe, kp)) * SM_SCALE

    cm = ckvm_ref[...]
    s = scores(cm, kpem_ref[...])
    m0 = jnp.max(s, axis=-1, keepdims=True)
    p = jnp.exp(s - m0)
    m_ref[...] = m0
    l_ref[...] = jnp.sum(p, axis=-1, keepdims=True)
    acc_ref[...] = jnp.dot(p.astype(BF16), cm, preferred_element_type=F32)

    def block_scores(j):
        k0 = pl.multiple_of(j * bq, bq)
        return scores(ckv_ref[pl.ds(k0, bq), :], kpe_ref[pl.ds(k0, bq), :])

    def absorb(s, j, masked):
        if masked:
            causal = (lax.broadcasted_iota(jnp.int32, (bq, bq), 0) >= lax.broadcasted_iota(jnp.int32, (bq, bq), 1))
            s = jnp.where(causal[None], s.reshape(MLA_HEADS, bq, bq), -jnp.inf).reshape(rows, bq)
        m_old = m_ref[...]
        m_new = jnp.maximum(m_old, jnp.max(s, axis=-1, keepdims=True))
        corr = jnp.exp(m_old - m_new)
        p = jnp.exp(s - m_new)
        m_ref[...] = m_new
        l_ref[...] = l_ref[...] * corr + jnp.sum(p, axis=-1, keepdims=True)
        c = ckv_ref[pl.ds(pl.multiple_of(j * bq, bq), bq), :]
        acc_ref[...] = acc_ref[...] * corr + jnp.dot(p.astype(BF16), c, preferred_element_type=F32)

    s_ref[...] = block_scores(0)

    def full_block(j, carry):
        s = s_ref[...]
        s_next = block_scores(j + 1)
        absorb(s, j, False)
        s_ref[...] = s_next
        return carry

    lax.fori_loop(0, qi, full_block, 0)
    absorb(s_ref[...], qi, True)
    o_ref[...] = _attn_out_proj(acc_ref[...] / l_ref[...], wuv_ref, g_ref, bq).astype(o_ref.dtype)


def _attn_prompt(qlat, qpe, ckvb, kpeb, ckvb_meta, kpeb_meta, w_uv_r, g_attn, bq):
    _, nb, seq, _ = qlat.shape
    assert seq % bq == 0
    rows = MLA_HEADS * bq
    return pl.pallas_call(
        functools.partial(_attn_prompt_kernel, bq=bq),
        grid=(nb, seq // bq),
        in_specs=[pl.BlockSpec((MLA_HEADS, None, bq, KV_LORA), lambda b, i: (0, b, i, 0)),
                  pl.BlockSpec((MLA_HEADS, None, bq, QK_ROPE), lambda b, i: (0, b, i, 0)),
                  pl.BlockSpec((None, seq, KV_LORA), lambda b, i: (b, 0, 0)),
                  pl.BlockSpec((None, seq, QK_ROPE), lambda b, i: (b, 0, 0)),
                  _const_spec((N_META, KV_LORA)), _const_spec((N_META, QK_ROPE)),
                  _const_spec((MLA_HEADS, KV_LORA, V_DIM)), _const_spec((1, MLA_INNER))],
        out_specs=pl.BlockSpec((None, bq, MLA_INNER), lambda b, i: (b, i, 0)),
        out_shape=jax.ShapeDtypeStruct((nb, seq, MLA_INNER), BF16),
        scratch_shapes=[pltpu.VMEM((rows, 1), F32), pltpu.VMEM((rows, 1), F32), pltpu.VMEM((rows, KV_LORA), F32),
                        pltpu.VMEM((rows, bq), F32)],
        compiler_params=_params(("arbitrary", "arbitrary")),
        name="attn_prompt",
    )(qlat, qpe, ckvb, kpeb, ckvb_meta, kpeb_meta, w_uv_r, g_attn)


HIST_ROWS = 8
GROUP_W = SSD_INNER // SSD_GROUPS


def _silu(x):
    return x * jax.nn.sigmoid(x)


def _softplus(x):
    return jnp.maximum(x, 0.0) + jnp.log1p(jnp.exp(-jnp.abs(x)))


def _split3_dot(m_bf16, a):
    hi = a.astype(BF16)
    r1 = a - hi.astype(F32)
    mid = r1.astype(BF16)
    lo = (r1 - mid.astype(F32)).astype(BF16)
    dot = lambda t: jnp.dot(m_bf16, t, preferred_element_type=F32)
    return dot(hi) + dot(mid) + dot(lo)


def _conv_silu(xpad_ref, xbc, hist_rows_ref, cw_ref, cb_ref, first, lc, valid):
    h0 = HIST_ROWS - (CONV_W - 1)

    @pl.when(first)
    def _():
        xpad_ref[h0:HIST_ROWS, :] = hist_rows_ref[...]

    xpad_ref[HIST_ROWS:HIST_ROWS + lc, :] = xbc
    acc = cb_ref[...] + cw_ref[CONV_W - 1:CONV_W, :] * xbc
    for k in range(CONV_W - 1):
        acc = acc + cw_ref[k:k + 1, :] * xpad_ref[h0 + k:h0 + k + lc, :]
    tail = xpad_ref[h0 + valid:HIST_ROWS + valid, :]
    xpad_ref[h0:HIST_ROWS, :] = tail
    return _silu(acc), tail


def _gated_group_norm(y, z, g):
    yg = y * _silu(z)
    parts = []
    for gi in range(SSD_GROUPS):
        part = yg[:, gi * GROUP_W:(gi + 1) * GROUP_W]
        parts.append(part * lax.rsqrt(jnp.mean(part * part, axis=-1, keepdims=True) + EPS))
    return jnp.concatenate(parts, axis=-1) * g


def _ssd_chunk_kernel(z_ref, xbc_ref, dt16_ref, dtx_ref, hist_ref, s0_ref, cw_ref, cb_ref,
                      dtb16_ref, alog16_ref, dtbx_ref, alogx_ref, dskx_ref, g_ref,
                      y_ref, sout_ref, cout_ref, st_ref, xpad_ref, *, lc, valid):
    c = pl.program_id(1)

    @pl.when(c == 0)
    def _():
        st_ref[...] = s0_ref[...].T

    xc, tail = _conv_silu(xpad_ref, xbc_ref[...], hist_ref, cw_ref, cb_ref, c == 0, lc, valid)
    xs = xc[:, :SSD_INNER]
    row_i = lax.broadcasted_iota(jnp.int32, (lc, lc), 0)
    col_j = lax.broadcasted_iota(jnp.int32, (lc, lc), 1)
    causal = row_i >= col_j
    tril = jnp.where(causal, 1.0, 0.0).astype(BF16)

    def dt_and_cumsum(raw_ref, bias_ref, alog_ref):
        dt = _softplus(raw_ref[...] + bias_ref[...])
        if valid < lc:
            dt = jnp.where(lax.broadcasted_iota(jnp.int32, dt.shape, 0) < valid, dt, 0.0)
        return dt, _split3_dot(tril, dt * (-jnp.exp(alog_ref[...])))

    _, acs16 = dt_and_cumsum(dt16_ref, dtb16_ref, alog16_ref)
    dtx, acsx = dt_and_cumsum(dtx_ref, dtbx_ref, alogx_ref)
    acs16_t = acs16.T
    xdt = (xs * dtx).astype(BF16)
    lane = lax.broadcasted_iota(jnp.int32, (lc, 2 * SSD_HEAD_DIM), 1)
    low_half = lane < SSD_HEAD_DIM
    zero = jnp.zeros((lc, 2 * SSD_HEAD_DIM), BF16)
    st = st_ref[...]
    decay_out = jnp.exp(acsx[lc - 1:lc, :] - acsx)
    xw = (xs * decay_out * dtx).astype(BF16)
    y_parts, st_parts = [], []
    for gi in range(SSD_GROUPS):
        b0 = SSD_INNER + gi * D_STATE
        c0 = SSD_INNER + SSD_GROUPS * D_STATE + gi * D_STATE
        bm = xc[:, b0:b0 + D_STATE].astype(BF16)
        cm = xc[:, c0:c0 + D_STATE].astype(BF16)
        cb = _nt_dot(cm, bm)
        lanes = slice(gi * GROUP_W, (gi + 1) * GROUP_W)
        y_off = jnp.dot(cm, st[:, lanes].astype(BF16), preferred_element_type=F32) * jnp.exp(acsx[:, lanes])
        for pair in range(SSD_HPG // 2):
            h0 = gi * SSD_HPG + 2 * pair
            pl_lanes = slice(h0 * SSD_HEAD_DIM, (h0 + 2) * SSD_HEAD_DIM)
            xpair = xdt[:, pl_lanes]
            y_pair = None
            for t in range(2):
                h = h0 + t
                seg = acs16[:, h:h + 1] - acs16_t[h:h + 1, :]
                m = (cb * jnp.exp(jnp.where(causal, seg, -jnp.inf))).astype(BF16)
                x_h = jnp.where(low_half, xpair, zero) if t == 0 else jnp.where(low_half, zero, xpair)
                y_h = jnp.dot(m, x_h, preferred_element_type=F32)
                y_pair = y_h if y_pair is None else y_pair + y_h
            y_parts.append(y_pair + y_off[:, pl_lanes.start - lanes.start:pl_lanes.stop - lanes.start])
        inc = lax.dot_general(bm, xw[:, lanes], (((0,), (0,)), ((), ())), preferred_element_type=F32)
        st_parts.append(st[:, lanes] * jnp.exp(acsx[lc - 1:lc, lanes]) + inc)
    st_new = jnp.concatenate(st_parts, axis=-1)
    st_ref[...] = st_new
    y = jnp.concatenate(y_parts, axis=-1) + dskx_ref[...] * xs
    y_ref[...] = _gated_group_norm(y, z_ref[...], g_ref[...]).astype(y_ref.dtype)

    @pl.when(c == pl.num_programs(1) - 1)
    def _():
        sout_ref[...] = st_new.T
        cout_ref[...] = tail


def _ssd_chunked(z, xbc, dt16, dtx, hist, s0, wts, lc, valid, shared_init):
    nb, t, _ = z.shape
    assert t % lc == 0
    tok = lambda w: pl.BlockSpec((None, lc, w), lambda b, c: (b, c, 0))
    init_idx = (lambda b, c: (0, 0, 0)) if shared_init else (lambda b, c: (b, 0, 0))
    per_b = lambda r, w: pl.BlockSpec((None, r, w), lambda b, c: (b, 0, 0))
    vec = lambda w: _const_spec((1, w))
    sds = jax.ShapeDtypeStruct
    return pl.pallas_call(
        functools.partial(_ssd_chunk_kernel, lc=lc, valid=valid),
        grid=(nb, t // lc),
        in_specs=[tok(SSD_INNER), tok(CONV_DIM), tok(LANES), tok(SSD_INNER),
                  pl.BlockSpec((None, CONV_W - 1, CONV_DIM), init_idx),
                  pl.BlockSpec((None, SSD_INNER, D_STATE), init_idx),
                  _const_spec((CONV_W, CONV_DIM)), vec(CONV_DIM), vec(LANES), vec(LANES),
                  vec(SSD_INNER), vec(SSD_INNER), vec(SSD_INNER), vec(SSD_INNER)],
        out_specs=[tok(SSD_INNER), per_b(SSD_INNER, D_STATE), per_b(CONV_W - 1, CONV_DIM)],
        out_shape=[sds((nb, t, SSD_INNER), BF16), sds((nb, SSD_INNER, D_STATE), F32),
                   sds((nb, CONV_W - 1, CONV_DIM), F32)],
        scratch_shapes=[pltpu.VMEM((D_STATE, SSD_INNER), F32), pltpu.VMEM((HIST_ROWS + lc, CONV_DIM), F32)],
        compiler_params=_params(("arbitrary", "arbitrary")),
        name="ssd_chunked",
    )(z, xbc, dt16, dtx, hist, s0, wts["conv_w"], wts["conv_b"], wts["dtb16"], wts["alog16"],
      wts["dtbx"], wts["alogx"], wts["dskx"], wts["g_ssd"])


def _ssd_step_kernel(z_ref, xbc_ref, dtx_ref, hist_ref, s0_ref, cw_ref, cb_ref, dtbx_ref, alogx_ref, dskx_ref, g_ref,
                     y_ref, sout_ref, cout_ref, *, nseq, lc):
    hist_n = CONV_W - 1
    t_idx = lax.broadcasted_iota(jnp.int32, (lc, SSD_INNER), 0)
    t_idx_g = lax.broadcasted_iota(jnp.int32, (lc, GROUP_W), 0)
    a_neg = -jnp.exp(alogx_ref[...])
    for b in range(nseq):
        xbc = xbc_ref[b]
        xp = jnp.concatenate([hist_ref[b], xbc], axis=0)
        acc = cb_ref[...] + cw_ref[hist_n:CONV_W, :] * xbc
        for k in range(hist_n):
            acc = acc + cw_ref[k:k + 1, :] * xp[k:k + lc]
        xc = _silu(acc)
        cout_ref[b] = xp[lc:lc + hist_n]
        xs = xc[:, :SSD_INNER]
        dt = _softplus(dtx_ref[b] + dtbx_ref[...])
        a = dt * a_neg
        acs = jnp.zeros_like(a)
        for l in range(lc):
            acs = acs + jnp.where(t_idx >= l, a[l:l + 1, :], 0.0)
        xdt = xs * dt
        st = s0_ref[b].T
        decay_last = jnp.exp(acs[lc - 1:lc, :])
        xw = (xs * jnp.exp(acs[lc - 1:lc, :] - acs) * dt).astype(BF16)
        y_parts, st_parts = [], []
        for gi in range(SSD_GROUPS):
            b0 = SSD_INNER + gi * D_STATE
            c0 = SSD_INNER + SSD_GROUPS * D_STATE + gi * D_STATE
            bm = xc[:, b0:b0 + D_STATE]
            cm = xc[:, c0:c0 + D_STATE]
            lanes = slice(gi * GROUP_W, (gi + 1) * GROUP_W)
            acs_g = acs[:, lanes]
            y_g = jnp.dot(cm.astype(BF16), st[:, lanes].astype(BF16), preferred_element_type=F32) * jnp.exp(acs_g)
            for j in range(lc):
                cb_j = jnp.sum(cm * bm[j:j + 1, :], axis=-1, keepdims=True)
                decay = jnp.exp(jnp.where(t_idx_g >= j, acs_g - acs_g[j:j + 1, :], -jnp.inf))
                y_g = y_g + cb_j * decay * xdt[j:j + 1, lanes]
            y_parts.append(y_g)
            inc = lax.dot_general(bm.astype(BF16), xw[:, lanes], (((0,), (0,)), ((), ())),
                                  preferred_element_type=F32)
            st_parts.append(st[:, lanes] * decay_last[:, lanes] + inc)
        sout_ref[b] = jnp.concatenate(st_parts, axis=-1).T
        y = jnp.concatenate(y_parts, axis=-1) + dskx_ref[...] * xs
        y_ref[b] = _gated_group_norm(y, z_ref[b], g_ref[...]).astype(y_ref.dtype)


def _ssd_step(z, xbc, dtx, hist, s0, wts, nseq):
    nb, lc, _ = z.shape
    assert nb % nseq == 0 and lc >= CONV_W - 1
    blk = lambda r, w: pl.BlockSpec((nseq, r, w), lambda i: (i, 0, 0))
    vec = lambda w: _const_spec((1, w))
    sds = jax.ShapeDtypeStruct
    return pl.pallas_call(
        functools.partial(_ssd_step_kernel, nseq=nseq, lc=lc),
        grid=(nb // nseq,),
        in_specs=[blk(lc, SSD_INNER), blk(lc, CONV_DIM), blk(lc, SSD_INNER), blk(CONV_W - 1, CONV_DIM),
                  blk(SSD_INNER, D_STATE), _const_spec((CONV_W, CONV_DIM)), vec(CONV_DIM),
                  vec(SSD_INNER), vec(SSD_INNER), vec(SSD_INNER), vec(SSD_INNER)],
        out_specs=[blk(lc, SSD_INNER), blk(SSD_INNER, D_STATE), blk(CONV_W - 1, CONV_DIM)],
        out_shape=[sds((nb, lc, SSD_INNER), F32), sds((nb, SSD_INNER, D_STATE), F32),
                   sds((nb, CONV_W - 1, CONV_DIM), F32)],
        compiler_params=_params(("arbitrary",)),
        name="ssd_step",
    )(z, xbc, dtx, hist, s0, wts["conv_w"], wts["conv_b"], wts["dtbx"], wts["alogx"], wts["dskx"], wts["g_ssd"])


def _prep_ssd_weights(conv_w, conv_b, dt_bias, a_log, d_skip, g_ssd_norm):
    pad16 = lambda v: jnp.pad(v.reshape(1, SSD_HEADS), ((0, 0), (0, LANES - SSD_HEADS)))
    expand = lambda v: jnp.repeat(v.reshape(1, SSD_HEADS), SSD_HEAD_DIM, axis=1)
    return {"conv_w": conv_w, "conv_b": conv_b.reshape(1, CONV_DIM), "dtb16": pad16(dt_bias), "alog16": pad16(a_log),
            "dtbx": expand(dt_bias), "alogx": expand(a_log), "dskx": expand(d_skip),
            "g_ssd": g_ssd_norm.reshape(1, SSD_INNER)}


PAGES_PER_STEP = 32


def _attn_sample_kernel(pt_ref, qlat_ref, qpe_ref, cnew_ref, knew_ref, wuv_ref, g_ref, ckv_hbm, kpe_hbm,
                        o_ref, m_ref, l_ref, acc_ref, cbuf, kbuf, sems, *, lq, page):
    b = pl.program_id(0)
    j = pl.program_id(1)
    nj = pl.num_programs(1)
    step = b * nj + j
    n_steps = pl.num_programs(0) * nj
    q_lat = qlat_ref[...]
    q_pe = qpe_ref[...]

    def page_copies(stp, slot, fn):
        sb = stp // nj
        sj = stp - sb * nj
        for p in range(PAGES_PER_STEP):
            pg = pt_ref[sb, sj * PAGES_PER_STEP + p]
            rows = pl.ds(p * page, page)
            fn(pltpu.make_async_copy(ckv_hbm.at[pg], cbuf.at[slot, rows], sems.at[slot]), p)
            fn(pltpu.make_async_copy(kpe_hbm.at[pg], kbuf.at[slot, :, rows], sems.at[slot]), p)

    start = lambda c, p: c.start(priority=p % 2)

    @pl.when(step == 0)
    def _():
        page_copies(0, 0, start)

    @pl.when(step + 1 < n_steps)
    def _():
        page_copies(step + 1, (step + 1) % 2, start)

    @pl.when(j == 0)
    def _():
        m_ref[...] = jnp.full(m_ref.shape, -jnp.inf, F32)
        l_ref[...] = jnp.zeros(l_ref.shape, F32)
        acc_ref[...] = jnp.zeros(acc_ref.shape, F32)

    def absorb(s, c):
        m_old = m_ref[...]
        m_new = jnp.maximum(m_old, jnp.max(s, axis=-1, keepdims=True))
        corr = jnp.exp(m_old - m_new)
        p = jnp.exp(s - m_new)
        m_ref[...] = m_new
        l_ref[...] = l_ref[...] * corr + jnp.sum(p, axis=-1, keepdims=True)
        acc_ref[...] = acc_ref[...] * corr + jnp.dot(p.astype(BF16), c, preferred_element_type=F32)

    slot = step % 2
    page_copies(step, slot, lambda c, p: c.wait())
    c = cbuf[slot].astype(BF16)
    kp_t = kbuf[slot].astype(BF16)
    absorb((_nt_dot(q_lat, c) + jnp.dot(q_pe, kp_t, preferred_element_type=F32)) * SM_SCALE, c)

    @pl.when(j == nj - 1)
    def _():
        rows = MLA_HEADS * lq
        c_new = cnew_ref[...].astype(BF16)
        s = (_nt_dot(q_lat, c_new) + _nt_dot(q_pe, knew_ref[...].astype(BF16))) * SM_SCALE
        q_idx = lax.broadcasted_iota(jnp.int32, (rows, lq), 0) % lq
        k_idx = lax.broadcasted_iota(jnp.int32, (rows, lq), 1)
        absorb(jnp.where(q_idx >= k_idx, s, -jnp.inf), c_new)
        o_ref[...] = _attn_out_proj(acc_ref[...] / l_ref[...], wuv_ref, g_ref, lq).astype(o_ref.dtype)


def _attn_sample(page_table, qlat, qpe, c_new, k_new, pool_ckv, pool_kpe, w_uv_r, g_attn):
    nb, rows, _ = qlat.shape
    lq = rows // MLA_HEADS
    n_pages = page_table.shape[1]
    page = pool_ckv.shape[1]
    assert n_pages % PAGES_PER_STEP == 0
    per_b = lambda r, w: pl.BlockSpec((None, r, w), lambda b, j, pt: (b, 0, 0))
    const = lambda shape: pl.BlockSpec(shape, lambda b, j, pt: (0,) * len(shape), pipeline_mode=pl.Buffered(1))
    keys = PAGES_PER_STEP * page
    return pl.pallas_call(
        functools.partial(_attn_sample_kernel, lq=lq, page=page),
        grid_spec=pltpu.PrefetchScalarGridSpec(
            num_scalar_prefetch=1,
            grid=(nb, n_pages // PAGES_PER_STEP),
            in_specs=[per_b(rows, KV_LORA), per_b(rows, QK_ROPE), per_b(lq, KV_LORA), per_b(lq, QK_ROPE),
                      const((MLA_HEADS, KV_LORA, V_DIM)), const((1, MLA_INNER)),
                      pl.BlockSpec(memory_space=pl.ANY), pl.BlockSpec(memory_space=pl.ANY)],
            out_specs=pl.BlockSpec((None, lq, MLA_INNER), lambda b, j, pt: (b, 0, 0)),
            scratch_shapes=[pltpu.VMEM((rows, 1), F32), pltpu.VMEM((rows, 1), F32), pltpu.VMEM((rows, KV_LORA), F32),
                            pltpu.VMEM((2, keys, KV_LORA), F32), pltpu.VMEM((2, QK_ROPE, keys), F32),
                            pltpu.SemaphoreType.DMA((2,))]),
        out_shape=jax.ShapeDtypeStruct((nb, lq, MLA_INNER), F32),
        compiler_params=_params(("arbitrary", "arbitrary")),
        name="attn_sample",
    )(page_table, qlat, qpe, c_new, k_new, w_uv_r, g_attn, pool_ckv, pool_kpe)


def _merge_router_kernel(ha_ref, ya_ref, oa_ref, hb_ref, yb_ref, ob_ref, wa_ref, wb_ref, gffn_ref, wr_ref, br_ref,
                         h2_ref, xn_ref, topi_ref, gate_ref, rank_ref, cnt_ref, carry_ref, *, tm, steps_a):
    i = pl.program_id(0)

    @pl.when(i == 0)
    def _():
        carry_ref[...] = jnp.zeros(carry_ref.shape, F32)

    first = i < steps_a
    pick = lambda a_ref, b_ref, dt: jnp.where(first, a_ref[...].astype(dt), b_ref[...].astype(dt))
    mix = (jnp.dot(pick(ya_ref, yb_ref, BF16), wa_ref[...], preferred_element_type=F32)
           + jnp.dot(pick(oa_ref, ob_ref, BF16), wb_ref[...], preferred_element_type=F32))
    h2 = pick(ha_ref, hb_ref, F32) + mix
    h2_ref[...] = h2
    xn32 = _rms(h2, gffn_ref[...])
    xn4 = _tile_view(xn_ref)
    for ct in range(ROW_TILES):
        xn4[:, ct] = xn32[:, ct * LANES:(ct + 1) * LANES].reshape(tm // SUBLANES, SUBLANES, LANES)
    xn = xn32.astype(BF16)
    lane = lax.broadcasted_iota(jnp.int32, (tm, LANES), 1)
    logits = jnp.dot(xn, wr_ref[...], preferred_element_type=F32) + br_ref[...]
    logits = jnp.where(lane < N_EXPERTS, logits, -jnp.inf)
    vals, idxs = [], []
    for _ in range(TOP_K):
        v = jnp.max(logits, axis=-1, keepdims=True)
        idx = jnp.min(jnp.where(logits == v, lane, LANES), axis=-1, keepdims=True)
        vals.append(v)
        idxs.append(idx)
        logits = jnp.where(lane == idx, -jnp.inf, logits)
    exps = [jnp.exp(v - vals[0]) for v in vals]
    denom = exps[0]
    for e in exps[1:]:
        denom = denom + e
    hit = [lane == idx for idx in idxs]
    onehot = jnp.zeros((tm, LANES), F32)
    for hk in hit:
        onehot = onehot + jnp.where(hk, 1.0, 0.0)
    r_i = lax.broadcasted_iota(jnp.int32, (tm, tm), 0)
    c_j = lax.broadcasted_iota(jnp.int32, (tm, tm), 1)
    strict_tril = jnp.where(r_i > c_j, 1.0, 0.0).astype(BF16)
    before = jnp.dot(strict_tril, onehot.astype(BF16), preferred_element_type=F32) + carry_ref[...]
    carry_ref[...] = carry_ref[...] + jnp.sum(onehot, axis=0, keepdims=True)
    topi = jnp.zeros((tm, LANES), jnp.int32)
    gate = jnp.zeros((tm, LANES), F32)
    rank = jnp.zeros((tm, LANES), jnp.int32)
    for k in range(TOP_K):
        rk = jnp.sum(jnp.where(hit[k], before, 0.0), axis=-1, keepdims=True).astype(jnp.int32)
        topi = jnp.where(lane == k, idxs[k], topi)
        gate = jnp.where(lane == k, exps[k] / denom, gate)
        rank = jnp.where(lane == k, rk, rank)
    topi_ref[...] = topi
    gate_ref[...] = gate
    rank_ref[...] = rank
    cnt_ref[...] = carry_ref[...]


def _merge_router(group_a, group_b, wts, tm):
    rows_a, rows_b = group_a[0].shape[0], group_b[0].shape[0]
    assert rows_a % tm == 0 and rows_b % tm == 0
    steps_a, steps_b = rows_a // tm, rows_b // tm
    rows = rows_a + rows_b
    spec_a = lambda w: pl.BlockSpec((tm, w), lambda i: (jnp.minimum(i, steps_a - 1), 0))
    spec_b = lambda w: pl.BlockSpec((tm, w), lambda i: (jnp.maximum(i - steps_a, 0), 0))
    row = lambda w: pl.BlockSpec((tm, w), lambda i: (i, 0))
    widths = (D_MODEL, SSD_INNER, MLA_INNER)
    sds = jax.ShapeDtypeStruct
    return pl.pallas_call(
        functools.partial(_merge_router_kernel, tm=tm, steps_a=steps_a),
        grid=(steps_a + steps_b,),
        in_specs=[spec_a(w) for w in widths] + [spec_b(w) for w in widths]
                 + [_const_spec((SSD_INNER, D_MODEL)), _const_spec((MLA_INNER, D_MODEL)), _const_spec((1, D_MODEL)),
                    _const_spec((D_MODEL, LANES)), _const_spec((1, LANES))],
        out_specs=[row(D_MODEL), _gatherable_rows_spec(tm, lambda i: i), row(LANES), row(LANES), row(LANES),
                   pl.BlockSpec((1, LANES), lambda i: (0, 0))],
        out_shape=[sds((rows, D_MODEL), F32), _gatherable_rows_shape(rows), sds((rows, LANES), jnp.int32),
                   sds((rows, LANES), F32), sds((rows, LANES), jnp.int32), sds((1, LANES), F32)],
        scratch_shapes=[pltpu.VMEM((1, LANES), F32)],
        compiler_params=_params(("arbitrary",)),
        name="merge_router",
    )(*group_a, *group_b, wts["w_out_a"], wts["w_out_b"], wts["g_ffn"], wts["w_router"], wts["b_router"])


MOE_TM = 1152
MOE_TS = 384
MOE_TF = 256
MOE_GATHER_ROWS = MOE_TM // (D_FF // MOE_TF)
COMBINE_TM = 256


def _moe_kernel(te_ref, nv_ref, src_ref, xn_hbm, wg_ref, wu_ref, wd_ref, bg_ref, bu_ref, bd_ref, y_ref,
                xland, xbf, wg_s, wu_s, wd_s, sem):
    del te_ref
    i = pl.program_id(0)
    f = pl.program_id(1)
    n_tiles = pl.num_programs(0)
    last_f = pl.num_programs(1) - 1
    nv = nv_ref[i]
    n_sub_max = MOE_TM // MOE_TS
    sub_tiles = MOE_TS // SUBLANES
    y4 = _tile_view(y_ref)
    land4 = _tile_view(xland)

    def row_copy(src_row, dst_row):
        return _gather_row_copy(xn_hbm, src_row, xland, dst_row, sem)

    def wait_tile_rows():
        for r in range(MOE_TM):
            row_copy(0, r).wait()

    @pl.when((i == 0) & (f == 0))
    def _():
        lax.fori_loop(0, MOE_TM, lambda r, c: (row_copy(src_ref[r], r).start(), c)[1], 0)

    requested = (i == 0) | (nv_ref[jnp.maximum(i - 1, 0)] > 0)

    @pl.when((f == 0) & requested)
    def _():
        wait_tile_rows()
        for s in range(n_sub_max):
            @pl.when(s * MOE_TS < nv)
            def _():
                for ct in range(ROW_TILES):
                    piece = land4[s * sub_tiles:(s + 1) * sub_tiles, ct].reshape(MOE_TS, LANES)
                    xbf[s * MOE_TS:(s + 1) * MOE_TS, ct * LANES:(ct + 1) * LANES] = piece.astype(BF16)

    for s in range(n_sub_max):
        tiles = slice(s * sub_tiles, (s + 1) * sub_tiles)

        @pl.when((f == 0) & (s * MOE_TS < nv))
        def _():
            for ct in range(ROW_TILES):
                y4[tiles, ct] = jnp.broadcast_to(bd_ref[:, ct * LANES:(ct + 1) * LANES], (sub_tiles, SUBLANES, LANES))

        @pl.when((f == 0) & (s * MOE_TS >= nv))
        def _():
            y4[tiles] = jnp.zeros((sub_tiles, ROW_TILES, SUBLANES, LANES), F32)

    def tile_step(n_sub):
        wg_s[...] = wg_ref[...].astype(BF16)
        wu_s[...] = wu_ref[...].astype(BF16)
        wd_s[...] = wd_ref[...].astype(BF16)
        for r in range(MOE_GATHER_ROWS):
            row = f * MOE_GATHER_ROWS + r
            row_copy(src_ref[(i + 1) * MOE_TM + row], row).start(priority=r % 2)
        for s in range(n_sub):
            x = xbf[s * MOE_TS:(s + 1) * MOE_TS, :]
            g = jnp.minimum(jnp.dot(x, wg_s[...], preferred_element_type=F32) + bg_ref[...], SWIGLU_LIMIT)
            u = jnp.clip(jnp.dot(x, wu_s[...], preferred_element_type=F32) + bu_ref[...],
                         -SWIGLU_LIMIT, SWIGLU_LIMIT)
            hdn = ((u + 1.0) * (g * jax.nn.sigmoid(SWIGLU_ALPHA * g))).astype(BF16)
            part = jnp.dot(hdn, wd_s[...], preferred_element_type=F32)
            tiles = slice(s * sub_tiles, (s + 1) * sub_tiles)
            for ct in range(ROW_TILES):
                y4[tiles, ct] = y4[tiles, ct] + part[:, ct * LANES:(ct + 1) * LANES].reshape(
                    sub_tiles, SUBLANES, LANES)

    n_sub_valid = (nv + MOE_TS - 1) // MOE_TS
    for k in range(1, n_sub_max + 1):
        pl.when(n_sub_valid == k)(functools.partial(tile_step, k))

    @pl.when((i == n_tiles - 1) & (f == last_f) & (nv > 0))
    def _():
        wait_tile_rows()


def _moe_experts(tile_expert, tile_nvalid, src_rows, xn_rows, w_gate, b_gate, w_up, b_up, w_down, b_down):
    n_tiles = src_rows.shape[0] // MOE_TM - 1
    n_f = D_FF // MOE_TF
    assert MOE_GATHER_ROWS * n_f == MOE_TM
    fidx = lambda i, f, nv: jnp.where(nv[i] > 0, f, n_f - 1)
    wspec = lambda shape, imap: pl.BlockSpec(shape, lambda i, f, te, nv, src: imap(te[i], fidx(i, f, nv)))
    return pl.pallas_call(
        _moe_kernel,
        grid_spec=pltpu.PrefetchScalarGridSpec(
            num_scalar_prefetch=3, grid=(n_tiles, n_f),
            in_specs=[pl.BlockSpec(memory_space=pl.ANY),
                      wspec((None, D_MODEL, MOE_TF), lambda e, f: (e, 0, f)),
                      wspec((None, D_MODEL, MOE_TF), lambda e, f: (e, 0, f)),
                      wspec((None, MOE_TF, D_MODEL), lambda e, f: (e, f, 0)),
                      wspec((None, 1, MOE_TF), lambda e, f: (e, 0, f)),
                      wspec((None, 1, MOE_TF), lambda e, f: (e, 0, f)),
                      wspec((None, 1, D_MODEL), lambda e, f: (e, 0, 0))],
            out_specs=_gatherable_rows_spec(MOE_TM, lambda i, f, te, nv, src: i),
            scratch_shapes=[_row_landing_scratch(MOE_TM), pltpu.VMEM((MOE_TM, D_MODEL), BF16),
                            pltpu.VMEM((D_MODEL, MOE_TF), BF16), pltpu.VMEM((D_MODEL, MOE_TF), BF16),
                            pltpu.VMEM((MOE_TF, D_MODEL), BF16), pltpu.SemaphoreType.DMA(())]),
        out_shape=_gatherable_rows_shape(n_tiles * MOE_TM),
        compiler_params=_params(("arbitrary", "arbitrary")),
        name="moe_experts",
    )(tile_expert, tile_nvalid, src_rows, xn_rows, w_gate, w_up, w_down,
      b_gate.reshape(N_EXPERTS, 1, D_FF), b_up.reshape(N_EXPERTS, 1, D_FF), b_down.reshape(N_EXPERTS, 1, D_MODEL))


def _combine_kernel(pos_ref, h2_ref, gate_ref, y_hbm, gfin_ref, out_ref, *scratch, tm, row_offset):
    bufs, sems = scratch[:2 * TOP_K], scratch[2 * TOP_K]
    i = pl.program_id(0)
    n = pl.num_programs(0)

    def row_copy(src_row, slot, k, t):
        return _gather_row_copy(y_hbm, src_row, bufs[slot * TOP_K + k], t, sems.at[slot])

    def start_copies(step, slot):
        def body(t, carry):
            for k in range(TOP_K):
                row_copy(pos_ref[(row_offset + step * tm + t) * TOP_K + k], slot, k, t).start(priority=k % 2)
            return carry
        lax.fori_loop(0, tm, body, 0, unroll=2)

    def wait_copies(slot):
        for t in range(tm):
            for k in range(TOP_K):
                row_copy(0, slot, k, t).wait()

    def for_slot(slot_value, fn):
        for slot in range(2):
            pl.when(slot_value == slot)(functools.partial(fn, slot))

    @pl.when(i == 0)
    def _():
        start_copies(0, 0)

    @pl.when(i + 1 < n)
    def _():
        for_slot((i + 1) % 2, lambda slot: start_copies(i + 1, slot))

    def finish(slot):
        wait_copies(slot)
        gate = gate_ref[...]
        gates = [jnp.broadcast_to(gate[:, k:k + 1], (tm, LANES)) for k in range(TOP_K)]
        views = [_tile_view(bufs[slot * TOP_K + k]) for k in range(TOP_K)]
        parts = []
        for ct in range(ROW_TILES):
            acc = h2_ref[:, ct * LANES:(ct + 1) * LANES]
            for k in range(TOP_K):
                acc = acc + gates[k] * views[k][:, ct].reshape(tm, LANES)
            parts.append(acc)
        out_ref[...] = _rms(jnp.concatenate(parts, axis=-1), gfin_ref[...])

    for_slot(i % 2, finish)


def _combine(pos_flat, h2, gate, y_rows, g_final, row_offset, rows):
    tm = COMBINE_TM
    assert rows % tm == 0 and row_offset % tm == 0
    off = row_offset // tm
    return pl.pallas_call(
        functools.partial(_combine_kernel, tm=tm, row_offset=row_offset),
        grid_spec=pltpu.PrefetchScalarGridSpec(
            num_scalar_prefetch=1, grid=(rows // tm,),
            in_specs=[pl.BlockSpec((tm, D_MODEL), lambda i, pos: (i + off, 0)),
                      pl.BlockSpec((tm, LANES), lambda i, pos: (i + off, 0)),
                      pl.BlockSpec(memory_space=pl.ANY),
                      pl.BlockSpec((1, D_MODEL), lambda i, pos: (0, 0))],
            out_specs=pl.BlockSpec((tm, D_MODEL), lambda i, pos: (i, 0)),
            scratch_shapes=[_row_landing_scratch(tm) for _ in range(2 * TOP_K)] + [pltpu.SemaphoreType.DMA((2,))]),
        out_shape=jax.ShapeDtypeStruct((rows, D_MODEL), F32),
        compiler_params=_params(("arbitrary",)),
        name="moe_combine",
    )(pos_flat, h2, gate, y_rows, g_final)


def _routing_tables(topi, rank, counts, n_rows):
    counts = counts[0, :N_EXPERTS].astype(jnp.int32)
    tiles_per = (counts + MOE_TM - 1) // MOE_TM
    tile_end = jnp.cumsum(tiles_per)
    tile_start = tile_end - tiles_per
    n_tiles_max = (n_rows * TOP_K + N_EXPERTS * (MOE_TM - 1)) // MOE_TM
    pos = tile_start[topi] * MOE_TM + rank
    src = jnp.zeros(((n_tiles_max + 1) * MOE_TM,), jnp.int32).at[pos.reshape(-1)].set(
        jnp.repeat(jnp.arange(n_rows, dtype=jnp.int32), TOP_K))
    tile_id = jnp.arange(n_tiles_max, dtype=jnp.int32)
    n_used = tile_end[-1]
    last = jnp.maximum(n_used - 1, 0)
    tile_row = jnp.minimum(tile_id, last)
    owner = jnp.sum((tile_end[None, :] <= tile_row[:, None]).astype(jnp.int32), axis=1)
    tile_expert = jnp.minimum(owner, N_EXPERTS - 1)
    in_tile = tile_row - tile_start[tile_expert]
    nvalid = jnp.clip(counts[tile_expert] - in_tile * MOE_TM, 0, MOE_TM)
    nvalid = jnp.where(tile_id < n_used, nvalid, 0).astype(jnp.int32)
    return pos.reshape(-1).astype(jnp.int32), src, tile_expert, nvalid


def kernel(x_prompt, x_sample, cache_ckv, cache_kpe, state_ssm, state_conv, page_table, meta_tokens, g_mix_norm, w_in, g_q_norm, w_q_up, g_kv_norm, w_uk, w_uv, conv_w, conv_b, dt_bias, a_log, d_skip, g_ssd_norm, g_attn_out, w_out, g_ffn_norm, w_router, b_router, w_gate, b_gate, w_up, b_up, w_down, b_down, g_final_norm):
    assert w_in.shape[0] == 1, "single-layer model"
    bp, seq = x_prompt.shape[:2]
    bs, dseq = x_sample.shape[:2]
    rows_p, rows_s = bp * seq, bs * dseq
    past_len = page_table.shape[1] * cache_ckv.shape[2]
    wm = _prep_mixer_weights(g_mix_norm[0], w_in[0], g_q_norm[0], w_q_up[0], g_kv_norm[0], w_uk[0])
    ws = _prep_ssd_weights(conv_w[0], conv_b[0], dt_bias[0], a_log[0], d_skip[0], g_ssd_norm[0])
    w_uv_r = jnp.transpose(w_uv[0], (1, 0, 2)).astype(BF16)
    g_attn = g_attn_out[0].reshape(1, MLA_INNER)
    wr = {"w_out_a": w_out[0][:SSD_INNER].astype(BF16), "w_out_b": w_out[0][SSD_INNER:].astype(BF16),
          "g_ffn": g_ffn_norm[0].reshape(1, D_MODEL),
          "w_router": jnp.pad(w_router[0], ((0, 0), (0, LANES - N_EXPERTS))).astype(BF16),
          "b_router": jnp.pad(b_router[0], (0, LANES - N_EXPERTS)).reshape(1, LANES)}

    tm = 256
    a_m = _inproj(meta_tokens, *_rope_tables(jnp.arange(N_META)), wm, N_META)
    a_p = _inproj(x_prompt.reshape(rows_p, D_MODEL), *_rope_tables(N_META + jnp.arange(seq)), wm, tm)
    cs_s = [jnp.tile(t, (tm // dseq, 1)) for t in _rope_tables(past_len + jnp.arange(dseq))]
    a_s = _inproj(x_sample.reshape(rows_s, D_MODEL), *cs_s, wm, tm)
    qlat_m, qpe_m, ckv_m, ckvb_m, kpe_m, kpeb_m, z_m, xbc_m, dt16_m, dtx_m = a_m
    qlat_p, qpe_p, ckv_p, ckvb_p, kpe_p, kpeb_p, z_p, xbc_p, dt16_p, dtx_p = a_p
    qlat_s, qpe_s, ckv_s, ckvb_s, kpe_s, kpeb_s, z_s, xbc_s, dt16_s, dtx_s = a_s
    del qlat_m, qpe_m, ckvb_s, kpeb_s, dt16_s

    heads_first = lambda q, b, t: q.reshape(MLA_HEADS, b, t, q.shape[-1])
    o_p = _attn_prompt(heads_first(qlat_p, bp, seq), heads_first(qpe_p, bp, seq),
                       ckvb_p.reshape(bp, seq, KV_LORA), kpeb_p.reshape(bp, seq, QK_ROPE),
                       ckvb_m, kpeb_m, w_uv_r, g_attn, 256)
    per_seq = lambda q: jnp.transpose(heads_first(q, bs, dseq), (1, 0, 2, 3)).reshape(bs, MLA_HEADS * dseq, q.shape[-1])
    o_s = _attn_sample(page_table, per_seq(qlat_s), per_seq(qpe_s), ckv_s.reshape(bs, dseq, KV_LORA),
                       kpe_s.reshape(bs, dseq, QK_ROPE), cache_ckv[0], jnp.swapaxes(cache_kpe[0], 1, 2),
                       w_uv_r, g_attn)

    pad_meta = lambda v: jnp.pad(v, ((0, CHUNK - N_META), (0, 0)))[None]
    _, ssm_m, conv_m = _ssd_chunked(pad_meta(z_m), pad_meta(xbc_m), pad_meta(dt16_m), pad_meta(dtx_m),
                                    jnp.zeros((1, CONV_W - 1, CONV_DIM), F32),
                                    jnp.zeros((1, SSD_INNER, D_STATE), F32), ws, CHUNK, N_META, True)
    per_b = lambda v, b, t: v.reshape(b, t, v.shape[-1])
    y_p, ssm_p, conv_p = _ssd_chunked(per_b(z_p, bp, seq), per_b(xbc_p, bp, seq), per_b(dt16_p, bp, seq),
                                      per_b(dtx_p, bp, seq), conv_m, ssm_m, ws, CHUNK, CHUNK, True)
    y_s, ssm_s, conv_s = _ssd_step(per_b(z_s, bs, dseq), per_b(xbc_s, bs, dseq), per_b(dtx_s, bs, dseq),
                                   state_conv[0], state_ssm[0].reshape(bs, SSD_INNER, D_STATE), ws, 8)

    total = rows_p + rows_s
    h2, xn_rows, topi, gate, rank, counts = _merge_router(
        (x_prompt.reshape(rows_p, D_MODEL), y_p.reshape(rows_p, SSD_INNER), o_p.reshape(rows_p, MLA_INNER)),
        (x_sample.reshape(rows_s, D_MODEL), y_s.reshape(rows_s, SSD_INNER), o_s.reshape(rows_s, MLA_INNER)),
        wr, tm)

    pos, src, tile_expert, tile_nvalid = _routing_tables(topi[:, :TOP_K], rank[:, :TOP_K], counts, total)
    y_rows = _moe_experts(tile_expert, tile_nvalid, src, xn_rows, w_gate[0], b_gate[0], w_up[0],
                          b_up[0], w_down[0], b_down[0])
    g_fin = g_final_norm.reshape(1, D_MODEL)
    y_prompt = _combine(pos, h2, gate, y_rows, g_fin, 0, rows_p).reshape(bp, seq, D_MODEL)
    y_sample = _combine(pos, h2, gate, y_rows, g_fin, rows_p, rows_s).reshape(bs, dseq, D_MODEL)

    with_meta = lambda m, p, w: jnp.concatenate(
        [jnp.broadcast_to(m[None], (bp, N_META, w)), p.reshape(bp, seq, w)], axis=1)[None]
    return (y_prompt, y_sample,
            with_meta(ckv_m, ckv_p, KV_LORA), with_meta(kpe_m, kpe_p, QK_ROPE),
            ssm_p.reshape(1, bp, SSD_HEADS, SSD_HEAD_DIM, D_STATE), conv_p[None],
            ckv_s.reshape(1, bs, dseq, KV_LORA), kpe_s.reshape(1, bs, dseq, QK_ROPE),
            ssm_s.reshape(1, bs, SSD_HEADS, SSD_HEAD_DIM, D_STATE), conv_s[None])
```
